```python
import math
import jax, jax.numpy as jnp
from jax import lax
import numpy as np

D_MODEL = 1024
BATCH = 4
SEQ = 4096
DEPTH = 1

PLE_DIM = 256
D_FF = 2816
C_CONV = D_MODEL
CONV_WIDTH = 31
HEAD_DIM = 128
HEADS_PER_GROUP = 4
GROUPS = ((128, 1), (512, 4), (2048, 16))
N_GROUPS = len(GROUPS)
N_HEADS = HEADS_PER_GROUP * N_GROUPS
ATTN_W = N_HEADS * HEAD_DIM
ATTN_OUT_W = HEADS_PER_GROUP * HEAD_DIM
Q_BLOCK = 128
ROPE_THETA = 10000.0
EPS = 1e-6
NEG = -1e30
IN_SPLITS = (2 * C_CONV, ATTN_W, ATTN_W, ATTN_W, 2 * D_MODEL)
IN_W = sum(IN_SPLITS)

kernel_name = "hybrid_gated_conv_dilated_attn_block"


def rms_norm(x, g):
    xf = x.astype(jnp.float32)
    y = xf * lax.rsqrt(jnp.mean(xf * xf, axis=-1, keepdims=True) + EPS)
    return (y * g.astype(jnp.float32)).astype(x.dtype)


def layer_norm(x, g, b):
    xf = x.astype(jnp.float32)
    mu = jnp.mean(xf, axis=-1, keepdims=True)
    var = jnp.mean(jnp.square(xf - mu), axis=-1, keepdims=True)
    y = (xf - mu) * lax.rsqrt(var + EPS)
    return (y * g.astype(jnp.float32) + b.astype(jnp.float32)).astype(x.dtype)


def swiglu(x, w_gate, w_up, w_down):
    return (jax.nn.silu(x @ w_gate) * (x @ w_up)) @ w_down


def rotary(t, cos, sin):
    t1, t2 = jnp.split(t, 2, axis=-1)
    return jnp.concatenate([t1 * cos - t2 * sin, t2 * cos + t1 * sin], axis=-1).astype(t.dtype)


def dilated_window_attention(q, k, v, dilation, steps):
    b, s, h, hd = q.shape
    L = s // dilation
    nb = -(-L // Q_BLOCK)
    Lp = nb * Q_BLOCK

    def to_strided(t):
        t = t.reshape(b, L, dilation, h, hd).transpose(0, 2, 3, 1, 4)
        t = jnp.pad(t, ((0, 0), (0, 0), (0, 0), (0, Lp - L), (0, 0)))
        return t.reshape(b, dilation, h, nb, Q_BLOCK, hd)

    def with_prev(t):
        prev = jnp.concatenate([jnp.zeros_like(t[:, :, :, :1]), t[:, :, :, :-1]], axis=3)
        return jnp.concatenate([prev, t], axis=4)

    qb = to_strided(q)
    kw = with_prev(to_strided(k))
    vw = with_prev(to_strided(v))
    scores = jnp.einsum('bdhnqc,bdhnkc->bdhnqk', qb, kw).astype(jnp.float32) * (hd ** -0.5)
    qi = jnp.arange(Q_BLOCK)[:, None]
    ki = jnp.arange(2 * Q_BLOCK)[None, :]
    dist = qi + Q_BLOCK - ki
    k_abs = jnp.arange(nb)[:, None, None] * Q_BLOCK - Q_BLOCK + ki[None]
    valid = (dist >= 0)[None] & (dist <= steps)[None] & (k_abs >= 0)
    scores = jnp.where(valid, scores, NEG)
    m = jnp.max(scores, axis=-1, keepdims=True)
    e = jnp.exp(scores - m)
    den = jnp.sum(e, axis=-1)
    o = jnp.einsum('bdhnqk,bdhnkc->bdhnqc', e, vw.astype(jnp.float32)) / den[..., None]
    lse = m[..., 0] + jnp.log(den)
    o = o.reshape(b, dilation, h, Lp, hd)[:, :, :, :L].transpose(0, 3, 1, 2, 4).reshape(b, s, h, hd)
    lse = lse.reshape(b, dilation, h, Lp)[..., :L].transpose(0, 3, 1, 2).reshape(b, s, h)
    return o.astype(q.dtype), lse


def conformer_conv(a_glu, w_dw, b_dw, g_ln, b_ln, w_out):
    a, gate = jnp.split(a_glu, 2, axis=-1)
    a = a * jax.nn.sigmoid(gate)
    a = lax.conv_general_dilated(
        a, w_dw[:, None, :].astype(a.dtype), window_strides=(1,),
        padding=[(CONV_WIDTH - 1, 0)],
        dimension_numbers=('NWC', 'WIO', 'NWC'),
        feature_group_count=C_CONV) + b_dw
    a = jax.nn.silu(layer_norm(a, g_ln, b_ln))
    return a @ w_out


def setup_inputs(seed: int = 0) -> dict:
    key = jax.random.key(seed)
    ks = iter(jax.random.split(key, 32))
    f32 = jnp.float32

    def w(shape, fan_in, scale=1.0):
        return jax.random.normal(next(ks), shape, f32) * (scale * fan_in ** -0.5)

    def gain(shape):
        return 1.0 + 0.02 * jax.random.normal(next(ks), shape, f32)

    def bias(shape):
        return 0.02 * jax.random.normal(next(ks), shape, f32)

    x = jax.random.normal(next(ks), (BATCH, SEQ, D_MODEL), f32)
    p = jax.random.normal(next(ks), (DEPTH, BATCH, SEQ, PLE_DIM), f32)
    positions = jnp.broadcast_to(jnp.arange(SEQ, dtype=jnp.int32)[None], (BATCH, SEQ))
    L = DEPTH
    return {
        "x": x, "p": p, "positions": positions,
        "g_ffn1": gain((L, D_MODEL)),
        "w_ffn1_gate": w((L, D_MODEL, D_FF), D_MODEL),
        "w_ffn1_up": w((L, D_MODEL, D_FF), D_MODEL),
        "w_ffn1_down": w((L, D_FF, D_MODEL), D_FF),
        "g_mix": gain((L, D_MODEL)),
        "w_in": w((L, D_MODEL, IN_W), D_MODEL),
        "w_dw": w((L, CONV_WIDTH, C_CONV), CONV_WIDTH),
        "b_dw": bias((L, C_CONV)),
        "g_conv_ln": gain((L, C_CONV)),
        "b_conv_ln": bias((L, C_CONV)),
        "w_conv_out": w((L, C_CONV, D_MODEL), C_CONV),
        "w_attn_out": w((L, ATTN_OUT_W, D_MODEL), ATTN_OUT_W),
        "w_mix_out": w((L, D_MODEL, D_MODEL), D_MODEL),
        "g_ffn2": gain((L, D_MODEL)),
        "w_ffn2_gate": w((L, D_MODEL, D_FF), D_MODEL),
        "w_ffn2_up": w((L, D_MODEL, D_FF), D_MODEL),
        "w_ffn2_down": w((L, D_FF, D_MODEL), D_FF),
        "g_ple": gain((L, D_MODEL)),
        "w_ple_gate": w((L, D_MODEL, D_MODEL), D_MODEL),
        "w_ple_proj": w((L, PLE_DIM, D_MODEL), PLE_DIM),
        "g_final": gain((D_MODEL,)),
    }


def reference(x, p, positions, g_ffn1, w_ffn1_gate, w_ffn1_up, w_ffn1_down, g_mix, w_in,
              w_dw, b_dw, g_conv_ln, b_conv_ln, w_conv_out, w_attn_out, w_mix_out,
              g_ffn2, w_ffn2_gate, w_ffn2_up, w_ffn2_down, g_ple, w_ple_gate, w_ple_proj,
              g_final):
    b, s, _ = x.shape
    inv_freq = ROPE_THETA ** (-jnp.arange(0, HEAD_DIM, 2, dtype=jnp.float32) / HEAD_DIM)
    ang = positions.astype(jnp.float32)[..., None] * inv_freq
    cos = jnp.cos(ang)[:, :, None, :].astype(x.dtype)
    sin = jnp.sin(ang)[:, :, None, :].astype(x.dtype)
    split_pts = list(np.cumsum(IN_SPLITS)[:-1])

    h = x
    for i in range(DEPTH):
        h = h + 0.5 * swiglu(rms_norm(h, g_ffn1[i]), w_ffn1_gate[i], w_ffn1_up[i], w_ffn1_down[i])

        u = rms_norm(h, g_mix[i])
        z = u @ w_in[i]
        a_glu, q, k, v, gate_logits = jnp.split(z, split_pts, axis=-1)

        y_conv = conformer_conv(a_glu, w_dw[i], b_dw[i], g_conv_ln[i], b_conv_ln[i], w_conv_out[i])

        q = rotary(q.reshape(b, s, N_HEADS, HEAD_DIM), cos, sin)
        k = rotary(k.reshape(b, s, N_HEADS, HEAD_DIM), cos, sin)
        v = v.reshape(b, s, N_HEADS, HEAD_DIM)
        outs, lses = [], []
        for gi, (window, dilation) in enumerate(GROUPS):
            hs = slice(gi * HEADS_PER_GROUP, (gi + 1) * HEADS_PER_GROUP)
            o_g, lse_g = dilated_window_attention(q[:, :, hs], k[:, :, hs], v[:, :, hs],
                                                  dilation, window // dilation)
            outs.append(o_g)
            lses.append(lse_g)
        alpha = jax.nn.softmax(jnp.stack(lses, axis=0), axis=0)
        o = jnp.sum(alpha[..., None].astype(q.dtype) * jnp.stack(outs, axis=0), axis=0)
        y_attn = o.reshape(b, s, ATTN_OUT_W) @ w_attn_out[i]

        g_a, g_b = jnp.split(jax.nn.sigmoid(gate_logits), 2, axis=-1)
        h = h + (g_a * y_conv + g_b * y_attn) @ w_mix_out[i]

        h = h + 0.5 * swiglu(rms_norm(h, g_ffn2[i]), w_ffn2_gate[i], w_ffn2_up[i], w_ffn2_down[i])

        gate = jax.nn.sigmoid(rms_norm(h, g_ple[i]) @ w_ple_gate[i])
        h = h + gate * (p[i] @ w_ple_proj[i])

    return rms_norm(h, g_final)
```

```python
import functools

import jax
import jax.numpy as jnp
from jax import lax
from jax.experimental import pallas as pl
from jax.experimental.pallas import tpu as pltpu

HEAD_DIM = 128
HEADS_PER_GROUP = 4
GROUPS = ((128, 1), (512, 4), (2048, 16))
N_GROUPS = len(GROUPS)
GROUP_W = HEADS_PER_GROUP * HEAD_DIM
ATTN_W = N_GROUPS * GROUP_W
CONV_WIDTH = 31
Q_BLOCK = 128
ROPE_THETA = 10000.0
EPS = 1e-6
NEG = -1e30

LANES = 128
CONV_HALO = 32
STAT_REP = LANES // HEADS_PER_GROUP
VMEM_LIMIT = 56 * 1024 * 1024

BF16 = jnp.bfloat16
F32 = jnp.float32


def _cparams(n_axes):
    return pltpu.CompilerParams(
        dimension_semantics=("arbitrary",) * n_axes, vmem_limit_bytes=VMEM_LIMIT)


def _resident(shape):
    nd = len(shape)
    return pl.BlockSpec(shape, lambda *_: (0,) * nd, pipeline_mode=pl.Buffered(1))


def _rms(x, g):
    ms = jnp.mean(x * x, axis=-1, keepdims=True)
    return x * lax.rsqrt(ms + EPS) * g


def _sigmoid(x):
    return 1.0 / (1.0 + jnp.exp(-x))


def _dot(a, b):
    return jnp.dot(a, b, preferred_element_type=F32)


def _swiglu_half_step(x, g, wg_ref, wu_ref, wd_ref):
    xn = _rms(x, g).astype(BF16)
    gate = _dot(xn, wg_ref[...])
    up = _dot(xn, wu_ref[...])
    act = (gate * _sigmoid(gate) * up).astype(BF16)
    return x + 0.5 * _dot(act, wd_ref[...])


def _rope_kernel(pos_ref, freq_ref, sign_ref, cos_ref, sin_ref):
    ang = pos_ref[...].astype(F32) * freq_ref[...]
    cos_ref[...] = jnp.cos(ang)
    sin_ref[...] = jnp.sin(ang) * sign_ref[...]


def _rope_tables(positions, tm):
    t = positions.size
    half = HEAD_DIM // 2
    inv_freq = ROPE_THETA ** (-jnp.arange(0, HEAD_DIM, 2, dtype=F32) / HEAD_DIM)
    freq = jnp.concatenate([inv_freq, inv_freq])[None, :]
    sign = jnp.concatenate([-jnp.ones((half,), F32), jnp.ones((half,), F32)])[None, :]
    tile = pl.BlockSpec((tm, HEAD_DIM), lambda i: (i, 0))
    const = pl.BlockSpec((1, HEAD_DIM), lambda i: (0, 0))
    return pl.pallas_call(
        _rope_kernel,
        grid=(t // tm,),
        in_specs=[pl.BlockSpec((tm, 1), lambda i: (i, 0)), const, const],
        out_specs=[tile, tile],
        out_shape=[jax.ShapeDtypeStruct((t, HEAD_DIM), F32)] * 2,
        compiler_params=_cparams(1),
        name="rope_tables",
    )(positions.reshape(t, 1), freq, sign)


def _ffn1_kernel(x_ref, g1_ref, wg_ref, wu_ref, wd_ref, gmix_ref, h_ref, u_ref):
    h = _swiglu_half_step(x_ref[...], g1_ref[...], wg_ref, wu_ref, wd_ref)
    h_ref[...] = h
    u_ref[...] = _rms(h, gmix_ref[...]).astype(BF16)


def _ffn1(x, g1, wg, wu, wd, gmix, tm):
    t, d = x.shape
    ff = wg.shape[1]
    tile = pl.BlockSpec((tm, d), lambda i: (i, 0))
    return pl.pallas_call(
        _ffn1_kernel,
        grid=(t // tm,),
        in_specs=[tile, _resident((1, d)), _resident((d, ff)), _resident((d, ff)),
                  _resident((ff, d)), _resident((1, d))],
        out_specs=[tile, tile],
        out_shape=[jax.ShapeDtypeStruct((t, d), F32), jax.ShapeDtypeStruct((t, d), BF16)],
        compiler_params=_cparams(1),
        name="ffn1",
    )(x, g1, wg, wu, wd, gmix)


def _rotary(t, cos, sin_signed):
    return t * cos + pltpu.roll(t, HEAD_DIM // 2, axis=1) * sin_signed


def _inproj_kernel(u_ref, w_ref, cos_ref, sin_ref,
                   a_ref, q0_ref, q1_ref, q2_ref, k0_ref, k1_ref, k2_ref,
                   v0_ref, v1_ref, v2_ref, ga_ref, gb_ref, *, c_conv, d_model):
    u = u_ref[...]
    cos = cos_ref[...]
    sin = sin_ref[...]

    def proj(lo, width):
        return _dot(u, w_ref[:, lo:lo + width])

    a = proj(0, c_conv)
    gate = proj(c_conv, c_conv)
    a_ref[...] = (a * _sigmoid(gate)).astype(BF16)

    base = 2 * c_conv
    scale = HEAD_DIM ** -0.5
    for g, (q_ref, k_ref, v_ref) in enumerate(
            ((q0_ref, k0_ref, v0_ref), (q1_ref, k1_ref, v1_ref), (q2_ref, k2_ref, v2_ref))):
        q = proj(base + g * GROUP_W, GROUP_W)
        k = proj(base + ATTN_W + g * GROUP_W, GROUP_W)
        for h in range(HEADS_PER_GROUP):
            sl = slice(h * HEAD_DIM, (h + 1) * HEAD_DIM)
            q_ref[:, sl] = (_rotary(q[:, sl], cos, sin) * scale).astype(BF16)
            k_ref[:, sl] = _rotary(k[:, sl], cos, sin).astype(BF16)
        v_ref[...] = proj(base + 2 * ATTN_W + g * GROUP_W, GROUP_W).astype(BF16)

    gbase = base + 3 * ATTN_W
    ga_ref[...] = _sigmoid(proj(gbase, d_model)).astype(BF16)
    gb_ref[...] = _sigmoid(proj(gbase + d_model, d_model)).astype(BF16)


def _inproj(u, w_in, cos, sin, c_conv, tm):
    t, d = u.shape
    in_w = w_in.shape[1]
    row = lambda w: pl.BlockSpec((tm, w), lambda i: (i, 0))
    sds = lambda w: jax.ShapeDtypeStruct((t, w), BF16)
    widths = [c_conv] + [GROUP_W] * 9 + [d, d]
    return pl.pallas_call(
        functools.partial(_inproj_kernel, c_conv=c_conv, d_model=d),
        grid=(t // tm,),
        in_specs=[row(d), _resident((d, in_w)), row(HEAD_DIM), row(HEAD_DIM)],
        out_specs=[row(w) for w in widths],
        out_shape=[sds(w) for w in widths],
        compiler_params=_cparams(1),
        name="inproj",
    )(u, w_in, cos, sin)


def _conv_kernel(a_ref, prev_ref, wdw_ref, bdw_ref, gln_ref, bln_ref, wout_ref, ga_ref,
                 out_ref, xbuf, *, ts):
    first = pl.program_id(1) == 0
    halo = prev_ref[0].astype(F32)
    xbuf[0:CONV_HALO, :] = jnp.where(first, 0.0, halo)
    xbuf[CONV_HALO:, :] = a_ref[0].astype(F32)
    c = xbuf.shape[1]
    acc = jnp.broadcast_to(bdw_ref[...], (ts, c))
    off = CONV_HALO - (CONV_WIDTH - 1)
    for j in range(CONV_WIDTH):
        acc = acc + wdw_ref[j:j + 1, :] * xbuf[off + j:off + j + ts, :]
    mu = jnp.mean(acc, axis=-1, keepdims=True)
    cen = acc - mu
    var = jnp.mean(cen * cen, axis=-1, keepdims=True)
    y = cen * lax.rsqrt(var + EPS) * gln_ref[...] + bln_ref[...]
    y = (y * _sigmoid(y)).astype(BF16)
    yc = _dot(y, wout_ref[...])
    out_ref[0] = (ga_ref[0].astype(F32) * yc).astype(BF16)


def _conv_branch(a, ga, w_dw, b_dw, g_ln, b_ln, w_out, ts):
    b, s, c = a.shape
    d = w_out.shape[1]
    per = ts // CONV_HALO
    tile = lambda w: pl.BlockSpec((1, ts, w), lambda bi, i: (bi, i, 0))
    prev = pl.BlockSpec((1, CONV_HALO, c), lambda bi, i: (bi, jnp.maximum(i * per - 1, 0), 0))
    return pl.pallas_call(
        functools.partial(_conv_kernel, ts=ts),
        grid=(b, s // ts),
        in_specs=[tile(c), prev, _resident((CONV_WIDTH, c)), _resident((1, c)),
                  _resident((1, c)), _resident((1, c)), _resident((c, d)), tile(d)],
        out_specs=tile(d),
        out_shape=jax.ShapeDtypeStruct((b, s, d), BF16),
        scratch_shapes=[pltpu.VMEM((ts + CONV_HALO, c), F32)],
        compiler_params=_cparams(2),
        name="conv_branch",
    )(a, a, w_dw, b_dw, g_ln, b_ln, w_out, ga)


def _attn_kernel(q_ref, k_ref, v_ref, kh_ref, vh_ref, o_ref, lse_ref, *, lt):
    first_tile = pl.program_id(2) == 0
    qi = lax.broadcasted_iota(jnp.int32, (Q_BLOCK, Q_BLOCK), 0)
    ki = lax.broadcasted_iota(jnp.int32, (Q_BLOCK, Q_BLOCK), 1)
    prev_band = ki >= qi
    cur_band = ki <= qi
    lane = lax.broadcasted_iota(jnp.int32, (Q_BLOCK, LANES), 1)
    nt = (((1,), (1,)), ((), ()))

    def block(n, rows, k_prev, v_prev, prev_ok):
        packed = jnp.zeros((Q_BLOCK, LANES), F32)
        for h in range(HEADS_PER_GROUP):
            sl = slice(h * HEAD_DIM, (h + 1) * HEAD_DIM)
            q = q_ref[0, rows, sl]
            s_cur = lax.dot_general(q, k_ref[0, rows, sl], nt, preferred_element_type=F32)
            s_prev = lax.dot_general(q, k_prev[:, sl], nt, preferred_element_type=F32)
            s_cur = jnp.where(cur_band, s_cur, NEG)
            s_prev = jnp.where(prev_band & prev_ok, s_prev, NEG)
            m = jnp.maximum(jnp.max(s_cur, axis=-1, keepdims=True),
                            jnp.max(s_prev, axis=-1, keepdims=True))
            e_cur = jnp.exp(s_cur - m)
            e_prev = jnp.exp(s_prev - m)
            den = jnp.sum(e_cur, axis=-1, keepdims=True) + jnp.sum(e_prev, axis=-1, keepdims=True)
            pv = (_dot(e_cur.astype(BF16), v_ref[0, rows, sl])
                  + _dot(e_prev.astype(BF16), v_prev[:, sl]))
            o_ref[0, rows, sl] = (pv / den).astype(BF16)
            lse = m + jnp.log(den)
            packed = jnp.where(lane >= h * STAT_REP, lse, packed)
        lse_ref[0, rows, :] = packed

    block(0, pl.ds(0, Q_BLOCK), kh_ref[0], vh_ref[0], jnp.logical_not(first_tile))

    def body(n, carry):
        rows = pl.ds(pl.multiple_of(n * Q_BLOCK, Q_BLOCK), Q_BLOCK)
        prows = pl.ds(pl.multiple_of((n - 1) * Q_BLOCK, Q_BLOCK), Q_BLOCK)
        block(n, rows, k_ref[0, prows, :], v_ref[0, prows, :], True)
        return carry

    lax.fori_loop(1, lt // Q_BLOCK, body, 0)


def _attention_group(q, k, v, dilation, lt):
    b, s, w = q.shape
    l = s // dilation
    lt = min(lt, l)
    per = lt // Q_BLOCK
    view = lambda x: x.reshape(b, l, dilation * x.shape[-1])
    tile = lambda width: pl.BlockSpec((1, lt, width), lambda bi, r, i: (bi, i, r))
    halo = pl.BlockSpec((1, Q_BLOCK, w), lambda bi, r, i: (bi, jnp.maximum(i * per - 1, 0), r))
    o, lse = pl.pallas_call(
        functools.partial(_attn_kernel, lt=lt),
        grid=(b, dilation, l // lt),
        in_specs=[tile(w), tile(w), tile(w), halo, halo],
        out_specs=[tile(w), tile(LANES)],
        out_shape=[jax.ShapeDtypeStruct((b, l, dilation * w), BF16),
                   jax.ShapeDtypeStruct((b, l, dilation * LANES), F32)],
        compiler_params=_cparams(3),
        name=f"attn_d{dilation}",
    )(view(q), view(k), view(v), view(k), view(v))
    return o.reshape(b, s, w), lse.reshape(b, s, LANES)


def _tail_kernel(h_ref, gyc_ref, gb_ref, o0_ref, o1_ref, o2_ref, l0_ref, l1_ref, l2_ref, p_ref,
                 wao_ref, wmix_ref, g2_ref, wg_ref, wu_ref, wd_ref, gple_ref, wpg_ref, wpp_ref,
                 gfin_ref, y_ref, *, tm, final_norm):
    l0, l1, l2 = l0_ref[...], l1_ref[...], l2_ref[...]
    m = jnp.maximum(jnp.maximum(l0, l1), l2)
    e0, e1, e2 = jnp.exp(l0 - m), jnp.exp(l1 - m), jnp.exp(l2 - m)
    inv = 1.0 / (e0 + e1 + e2)
    alphas = (e0 * inv, e1 * inv, e2 * inv)
    o_refs = (o0_ref, o1_ref, o2_ref)
    heads = []
    for h in range(HEADS_PER_GROUP):
        sl = slice(h * HEAD_DIM, (h + 1) * HEAD_DIM)
        acc = None
        for g in range(N_GROUPS):
            a = jnp.broadcast_to(alphas[g][:, h * STAT_REP:h * STAT_REP + 1], (tm, HEAD_DIM))
            term = a * o_refs[g][:, sl].astype(F32)
            acc = term if acc is None else acc + term
        heads.append(acc.astype(BF16))
    o = jnp.concatenate(heads, axis=1)
    y_attn = _dot(o, wao_ref[...])
    mixed = gyc_ref[...].astype(F32) + gb_ref[...].astype(F32) * y_attn
    h = h_ref[...] + _dot(mixed.astype(BF16), wmix_ref[...])
    h = _swiglu_half_step(h, g2_ref[...], wg_ref, wu_ref, wd_ref)
    gate = _sigmoid(_dot(_rms(h, gple_ref[...]).astype(BF16), wpg_ref[...]))
    h = h + gate * _dot(p_ref[...].astype(BF16), wpp_ref[...])
    y_ref[...] = _rms(h, gfin_ref[...]) if final_norm else h


def _tail(h, gyc, gb, os_, ls_, p, wao, wmix, g2, wg, wu, wd, gple, wpg, wpp, gfin, tm,
          final_norm):
    t, d = h.shape
    row = lambda w: pl.BlockSpec((tm, w), lambda i: (i, 0))
    res = lambda x: _resident(x.shape)
    weights = (wao, wmix, g2, wg, wu, wd, gple, wpg, wpp, gfin)
    return pl.pallas_call(
        functools.partial(_tail_kernel, tm=tm, final_norm=final_norm),
        grid=(t // tm,),
        in_specs=[row(d), row(d), row(d)] + [row(GROUP_W)] * 3 + [row(LANES)] * 3
                 + [row(p.shape[1])] + [res(x) for x in weights],
        out_specs=row(d),
        out_shape=jax.ShapeDtypeStruct((t, d), F32),
        compiler_params=_cparams(1),
        name="tail",
    )(h, gyc, gb, *os_, *ls_, p, *weights)


def kernel(x, p, positions, g_ffn1, w_ffn1_gate, w_ffn1_up, w_ffn1_down, g_mix, w_in,
           w_dw, b_dw, g_conv_ln, b_conv_ln, w_conv_out, w_attn_out, w_mix_out,
           g_ffn2, w_ffn2_gate, w_ffn2_up, w_ffn2_down, g_ple, w_ple_gate, w_ple_proj,
           g_final):
    b, s, d = x.shape
    depth = p.shape[0]
    t = b * s
    c_conv = w_dw.shape[-1]
    assert w_in.shape[-1] == 2 * c_conv + 3 * ATTN_W + 2 * d
    tm = 512
    assert t % tm == 0 and s % tm == 0
    for _, dil in GROUPS:
        assert (s // dil) % Q_BLOCK == 0

    cos, sin = _rope_tables(positions, tm)
    row = lambda v: v.reshape(1, -1)
    bf = lambda w: w.astype(BF16)

    h = x.reshape(t, d)
    for i in range(depth):
        h, u = _ffn1(h, row(g_ffn1[i]), bf(w_ffn1_gate[i]), bf(w_ffn1_up[i]),
                     bf(w_ffn1_down[i]), row(g_mix[i]), tm)
        a, q0, q1, q2, k0, k1, k2, v0, v1, v2, ga, gb = _inproj(u, bf(w_in[i]), cos, sin, c_conv, tm)
        gyc = _conv_branch(a.reshape(b, s, c_conv), ga.reshape(b, s, d), w_dw[i], row(b_dw[i]),
                           row(g_conv_ln[i]), row(b_conv_ln[i]), bf(w_conv_out[i]), 256)
        os_, ls_ = [], []
        for (_, dil), q, k, v in zip(GROUPS, (q0, q1, q2), (k0, k1, k2), (v0, v1, v2)):
            o, lse = _attention_group(q.reshape(b, s, GROUP_W), k.reshape(b, s, GROUP_W),
                                      v.reshape(b, s, GROUP_W), dil, 1024)
            os_.append(o.reshape(t, GROUP_W))
            ls_.append(lse.reshape(t, LANES))
        h = _tail(h, gyc.reshape(t, d), gb, os_, ls_, p[i].reshape(t, -1),
                  bf(w_attn_out[i]), bf(w_mix_out[i]), row(g_ffn2[i]), bf(w_ffn2_gate[i]),
                  bf(w_ffn2_up[i]), bf(w_ffn2_down[i]), row(g_ple[i]), bf(w_ple_gate[i]),
                  bf(w_ple_proj[i]), row(g_final), tm, i == depth - 1)
    return h.reshape(b, s, d)
```

```python
import functools

import jax
import jax.numpy as jnp
from jax import lax
from jax.experimental import pallas as pl
from jax.experimental.pallas import tpu as pltpu

HEAD_DIM = 128
HEADS_PER_GROUP = 4
GROUPS = ((128, 1), (512, 4), (2048, 16))
N_GROUPS = len(GROUPS)
GROUP_W = HEADS_PER_GROUP * HEAD_DIM
ATTN_W = N_GROUPS * GROUP_W
CONV_WIDTH = 31
Q_BLOCK = 128
ROPE_THETA = 10000.0
EPS = 1e-6
NEG = -1e30

LANES = 128
SUBLANES = 8
CONV_HALO = 32
CONV_ROWS = 64
STAT_REP = LANES // HEADS_PER_GROUP
VMEM_LIMIT = 56 * 1024 * 1024

BF16 = jnp.bfloat16
F32 = jnp.float32


def _cparams(n_axes):
    return pltpu.CompilerParams(
        dimension_semantics=("arbitrary",) * n_axes, vmem_limit_bytes=VMEM_LIMIT)


def _resident(shape):
    nd = len(shape)
    return pl.BlockSpec(shape, lambda *_: (0,) * nd, pipeline_mode=pl.Buffered(1))


def _rms(x, g):
    ms = jnp.mean(x * x, axis=-1, keepdims=True)
    return x * lax.rsqrt(ms + EPS) * g


def _sigmoid(x):
    return 1.0 / (1.0 + jnp.exp(-x))


def _dot(a, b):
    return jnp.dot(a, b, preferred_element_type=F32)


def _swiglu_half_step(x, g, wg_ref, wu_ref, wd_ref):
    xn = _rms(x, g).astype(BF16)
    gate = _dot(xn, wg_ref[...])
    up = _dot(xn, wu_ref[...])
    act = (gate * _sigmoid(gate) * up).astype(BF16)
    return x + 0.5 * _dot(act, wd_ref[...])


def _rope_kernel(pos_ref, freq_ref, sign_ref, cos_ref, sin_ref):
    ang = pos_ref[...].astype(F32) * freq_ref[...]
    cos_ref[...] = jnp.cos(ang)
    sin_ref[...] = jnp.sin(ang) * sign_ref[...]


def _rope_tables(positions, tm):
    t = positions.size
    half = HEAD_DIM // 2
    inv_freq = ROPE_THETA ** (-jnp.arange(0, HEAD_DIM, 2, dtype=F32) / HEAD_DIM)
    freq = jnp.concatenate([inv_freq, inv_freq])[None, :]
    sign = jnp.concatenate([-jnp.ones((half,), F32), jnp.ones((half,), F32)])[None, :]
    tile = pl.BlockSpec((tm, HEAD_DIM), lambda i: (i, 0))
    const = pl.BlockSpec((1, HEAD_DIM), lambda i: (0, 0))
    return pl.pallas_call(
        _rope_kernel,
        grid=(t // tm,),
        in_specs=[pl.BlockSpec((tm, 1), lambda i: (i, 0)), const, const],
        out_specs=[tile, tile],
        out_shape=[jax.ShapeDtypeStruct((t, HEAD_DIM), F32)] * 2,
        compiler_params=_cparams(1),
        name="rope_tables",
    )(positions.reshape(t, 1), freq, sign)


def _ffn1_kernel(x_ref, g1_ref, wg_ref, wu_ref, wd_ref, gmix_ref, h_ref, u_ref):
    h = _swiglu_half_step(x_ref[...], g1_ref[...], wg_ref, wu_ref, wd_ref)
    h_ref[...] = h
    u_ref[...] = _rms(h, gmix_ref[...]).astype(BF16)


def _ffn1(x, g1, wg, wu, wd, gmix, tm):
    t, d = x.shape
    ff = wg.shape[1]
    tile = pl.BlockSpec((tm, d), lambda i: (i, 0))
    return pl.pallas_call(
        _ffn1_kernel,
        grid=(t // tm,),
        in_specs=[tile, _resident((1, d)), _resident((d, ff)), _resident((d, ff)),
                  _resident((ff, d)), _resident((1, d))],
        out_specs=[tile, tile],
        out_shape=[jax.ShapeDtypeStruct((t, d), F32), jax.ShapeDtypeStruct((t, d), BF16)],
        compiler_params=_cparams(1),
        name="ffn1",
    )(x, g1, wg, wu, wd, gmix)


def _rotary(t, cos, sin_signed):
    return t * cos + pltpu.roll(t, HEAD_DIM // 2, axis=1) * sin_signed


def _conv_lane_chunk(xbuf, w_ref, b_ref, acc_ref, tm, c):
    off = CONV_HALO - (CONV_WIDTH - 1)
    cols = slice(c * LANES, (c + 1) * LANES)
    taps = [w_ref[j:j + 1, cols] for j in range(CONV_WIDTH)]
    bias = b_ref[:, cols]
    for rb in range(tm // CONV_ROWS):
        base = rb * CONV_ROWS
        out = None
        for r in range(SUBLANES):
            n = CONV_ROWS if r == 0 else CONV_ROWS + SUBLANES
            part = None
            for q in range((off + CONV_WIDTH - 1) // SUBLANES + 1):
                j = SUBLANES * q + r - off
                if 0 <= j < CONV_WIDTH:
                    lo = base + SUBLANES * q
                    term = taps[j] * xbuf[lo:lo + n, cols]
                    part = term if part is None else part + term
            shifted = part if r == 0 else part[r:r + CONV_ROWS]
            out = shifted if out is None else out + shifted
        acc_ref[base:base + CONV_ROWS, cols] = out + bias


def _mixer_in_kernel(u_ref, w_ref, cos_ref, sin_ref, wdw_ref, bdw_ref, gln_ref, bln_ref, wco_ref,
                     gyc_ref, q0_ref, q1_ref, q2_ref, k0_ref, k1_ref, k2_ref,
                     v0_ref, v1_ref, v2_ref, gb_ref, xbuf, acc_ref, ga_buf, stage,
                     *, tm, c_conv, d_model):
    def proj(lo, width):
        return _dot(u_ref[...], w_ref[:, lo:lo + width])

    @pl.when(pl.program_id(1) == 0)
    def _():
        xbuf[0:CONV_HALO, :] = jnp.zeros((CONV_HALO, c_conv), F32)

    a = proj(0, c_conv)
    gate = proj(c_conv, c_conv)
    xbuf[CONV_HALO:, :] = a * _sigmoid(gate)

    def emit(out_ref, val, dil):
        if dil == 1:
            out_ref[...] = val.astype(BF16)
            return
        rows = tm // dil
        for c in range(GROUP_W // LANES):
            stage[c] = val[:, c * LANES:(c + 1) * LANES]
        for r in range(dil):
            for c in range(GROUP_W // LANES):
                lo = r * GROUP_W + c * LANES
                out_ref[:, lo:lo + LANES] = stage[c, pl.ds(r, rows, stride=dil), :].astype(BF16)

    base = 2 * c_conv
    gbase = base + 3 * ATTN_W

    def rotary_piece(out_ref, lo, dil, scale):
        def piece():
            cos, sin = cos_ref[...], sin_ref[...]
            t = proj(lo, GROUP_W)
            heads = [_rotary(t[:, h * HEAD_DIM:(h + 1) * HEAD_DIM], cos, sin)
                     for h in range(HEADS_PER_GROUP)]
            if scale is not None:
                heads = [x * scale for x in heads]
            emit(out_ref, jnp.concatenate(heads, axis=1), dil)
        return piece

    def value_piece(out_ref, lo, dil):
        return lambda: emit(out_ref, proj(lo, GROUP_W), dil)

    def gate_a_piece():
        ga_buf[...] = _sigmoid(proj(gbase, d_model))

    def gate_b_piece():
        gb_ref[...] = _sigmoid(proj(gbase + d_model, d_model)).astype(BF16)

    pieces = [gate_a_piece, gate_b_piece]
    for g, (q_ref, k_ref, v_ref) in enumerate(
            ((q0_ref, k0_ref, v0_ref), (q1_ref, k1_ref, v1_ref), (q2_ref, k2_ref, v2_ref))):
        dil = GROUPS[g][1]
        pieces.append(rotary_piece(q_ref, base + g * GROUP_W, dil, HEAD_DIM ** -0.5))
        pieces.append(rotary_piece(k_ref, base + ATTN_W + g * GROUP_W, dil, None))
        pieces.append(value_piece(v_ref, base + 2 * ATTN_W + g * GROUP_W, dil))

    for c in range(c_conv // LANES):
        _conv_lane_chunk(xbuf, wdw_ref, bdw_ref, acc_ref, tm, c)
    xbuf[0:CONV_HALO, :] = xbuf[tm:tm + CONV_HALO, :]
    for piece in pieces:
        piece()

    acc = acc_ref[...]
    mu = jnp.mean(acc, axis=-1, keepdims=True)
    cen = acc - mu
    var = jnp.mean(cen * cen, axis=-1, keepdims=True)
    y = cen * lax.rsqrt(var + EPS) * gln_ref[...] + bln_ref[...]
    y = (y * _sigmoid(y)).astype(BF16)
    gyc_ref[...] = (ga_buf[...] * _dot(y, wco_ref[...])).astype(BF16)


def _mixer_in(u, w_in, cos, sin, w_dw, b_dw, g_ln, b_ln, w_co, batch, tm):
    t, d = u.shape
    in_w = w_in.shape[1]
    c_conv = w_dw.shape[1]
    per_seq = t // batch // tm
    idx = lambda b, i: (b * per_seq + i, 0)
    row = lambda w: pl.BlockSpec((tm, w), idx)
    strided = lambda dil: pl.BlockSpec((tm // dil, dil * GROUP_W), idx)
    strided_sds = lambda dil: jax.ShapeDtypeStruct((t // dil, dil * GROUP_W), BF16)
    dils = [dil for _, dil in GROUPS]
    out_specs = [row(d)] + [strided(dil) for _ in range(3) for dil in dils] + [row(d)]
    out_shape = ([jax.ShapeDtypeStruct((t, d), BF16)]
                 + [strided_sds(dil) for _ in range(3) for dil in dils]
                 + [jax.ShapeDtypeStruct((t, d), BF16)])
    return pl.pallas_call(
        functools.partial(_mixer_in_kernel, tm=tm, c_conv=c_conv, d_model=d),
        grid=(batch, per_seq),
        in_specs=[row(d), _resident((d, in_w)), row(HEAD_DIM), row(HEAD_DIM),
                  _resident((CONV_WIDTH, c_conv)), _resident((1, c_conv)), _resident((1, c_conv)),
                  _resident((1, c_conv)), _resident((c_conv, d))],
        out_specs=out_specs,
        out_shape=out_shape,
        scratch_shapes=[pltpu.VMEM((tm + CONV_HALO, c_conv), F32),
                        pltpu.VMEM((tm, c_conv), F32),
                        pltpu.VMEM((tm, d), F32),
                        pltpu.VMEM((GROUP_W // LANES, tm, LANES), F32)],
        compiler_params=_cparams(2),
        name="mixer_in",
    )(u, w_in, cos, sin, w_dw, b_dw, g_ln, b_ln, w_co)


def _attn_kernel(q_ref, k_ref, v_ref, kh_ref, vh_ref, o_ref, lse_ref, kbuf, vbuf, *, lt):
    first_tile = pl.program_id(2) == 0
    kbuf[0:Q_BLOCK, :] = kh_ref[0]
    kbuf[Q_BLOCK:, :] = k_ref[0]
    ones = jnp.ones((lt + Q_BLOCK, HEAD_DIM), BF16)
    for h in range(HEADS_PER_GROUP):
        sl = slice(h * HEAD_DIM, (h + 1) * HEAD_DIM)
        vbuf[0:Q_BLOCK, 2 * h * HEAD_DIM:(2 * h + 1) * HEAD_DIM] = vh_ref[0, :, sl]
        vbuf[Q_BLOCK:, 2 * h * HEAD_DIM:(2 * h + 1) * HEAD_DIM] = v_ref[0, :, sl]
        vbuf[:, (2 * h + 1) * HEAD_DIM:(2 * h + 2) * HEAD_DIM] = ones

    qi = lax.broadcasted_iota(jnp.int32, (Q_BLOCK, 2 * Q_BLOCK), 0)
    ki = lax.broadcasted_iota(jnp.int32, (Q_BLOCK, 2 * Q_BLOCK), 1)
    dist = qi + Q_BLOCK - ki
    band = (dist >= 0) & (dist <= Q_BLOCK)
    lane = lax.broadcasted_iota(jnp.int32, (Q_BLOCK, LANES), 1)
    nt = (((1,), (1,)), ((), ()))

    def block(start, mask):
        rows = pl.ds(start, Q_BLOCK)
        keys = pl.ds(start, 2 * Q_BLOCK)
        heads = range(HEADS_PER_GROUP)
        sls = [slice(h * HEAD_DIM, (h + 1) * HEAD_DIM) for h in heads]
        s = [lax.dot_general(q_ref[0, rows, sls[h]], kbuf[keys, sls[h]], nt,
                             preferred_element_type=F32) for h in heads]
        s = [jnp.where(mask, x, NEG) for x in s]
        m = [jnp.max(x, axis=-1, keepdims=True) for x in s]
        e = [jnp.exp(x - mx).astype(BF16) for x, mx in zip(s, m)]
        pv = [_dot(e[h], vbuf[keys, 2 * h * HEAD_DIM:(2 * h + 2) * HEAD_DIM]) for h in heads]
        packed = None
        for h in heads:
            den = pv[h][:, HEAD_DIM:]
            o_ref[0, rows, sls[h]] = (pv[h][:, :HEAD_DIM] / den).astype(BF16)
            lse = m[h] + jnp.log(den)
            packed = lse if packed is None else jnp.where(lane >= h * STAT_REP, lse, packed)
        lse_ref[0, rows, :] = packed

    block(0, band & jnp.logical_or(ki >= Q_BLOCK, jnp.logical_not(first_tile)))

    def body(n, carry):
        block(pl.multiple_of(n * Q_BLOCK, Q_BLOCK), band)
        return carry

    lax.fori_loop(1, lt // Q_BLOCK, body, 0)


def _attention_group(q, k, v, batch, dilation, lt):
    rows, _ = q.shape
    l = rows // batch
    lt = min(lt, l)
    per = lt // Q_BLOCK
    view = lambda x: x.reshape(batch, l, x.shape[-1])
    tile = lambda width: pl.BlockSpec((1, lt, width), lambda bi, r, i: (bi, i, r))
    halo = pl.BlockSpec((1, Q_BLOCK, GROUP_W), lambda bi, r, i: (bi, jnp.maximum(i * per - 1, 0), r))
    o, lse = pl.pallas_call(
        functools.partial(_attn_kernel, lt=lt),
        grid=(batch, dilation, l // lt),
        in_specs=[tile(GROUP_W), tile(GROUP_W), tile(GROUP_W), halo, halo],
        out_specs=[tile(GROUP_W), tile(LANES)],
        out_shape=[jax.ShapeDtypeStruct((batch, l, dilation * GROUP_W), BF16),
                   jax.ShapeDtypeStruct((batch, l, dilation * LANES), F32)],
        scratch_shapes=[pltpu.VMEM((lt + Q_BLOCK, GROUP_W), BF16),
                        pltpu.VMEM((lt + Q_BLOCK, 2 * GROUP_W), BF16)],
        compiler_params=_cparams(3),
        name=f"attn_d{dilation}",
    )(view(q), view(k), view(v), view(k), view(v))
    return o.reshape(rows, dilation * GROUP_W), lse.reshape(rows, dilation * LANES)


def _tail_kernel(h_ref, gyc_ref, gb_ref, o0_ref, o1_ref, o2_ref, l0_ref, l1_ref, l2_ref, p_ref,
                 wao_ref, wmix_ref, g2_ref, wg_ref, wu_ref, wd_ref, gple_ref, wpg_ref, wpp_ref,
                 gfin_ref, y_ref, obuf, lbuf, *, tm, final_norm):
    def token_order(dst, src_ref, dil, width):
        chunks = width // LANES
        if dil == 1:
            return [src_ref[:, c * LANES:(c + 1) * LANES].astype(F32) for c in range(chunks)]
        rows = tm // dil
        for r in range(dil):
            for c in range(chunks):
                lo = r * width + c * LANES
                dst[c, pl.ds(r, rows, stride=dil), :] = src_ref[:, lo:lo + LANES].astype(F32)
        return [dst[c] for c in range(chunks)]

    dils = [dil for _, dil in GROUPS]
    ls = [token_order(lbuf.at[g], ref, dils[g], LANES)[0]
          for g, ref in enumerate((l0_ref, l1_ref, l2_ref))]
    os_ = [token_order(obuf.at[g], ref, dils[g], GROUP_W)
           for g, ref in enumerate((o0_ref, o1_ref, o2_ref))]

    m = jnp.maximum(jnp.maximum(ls[0], ls[1]), ls[2])
    es = [jnp.exp(x - m) for x in ls]
    inv = 1.0 / (es[0] + es[1] + es[2])
    alphas = [x * inv for x in es]
    heads = []
    for h in range(HEADS_PER_GROUP):
        acc = None
        for g in range(N_GROUPS):
            a = jnp.broadcast_to(alphas[g][:, h * STAT_REP:h * STAT_REP + 1], (tm, HEAD_DIM))
            term = a * os_[g][h]
            acc = term if acc is None else acc + term
        heads.append(acc.astype(BF16))
    o = jnp.concatenate(heads, axis=1)
    y_attn = _dot(o, wao_ref[...])
    mixed = gyc_ref[...].astype(F32) + gb_ref[...].astype(F32) * y_attn
    h = h_ref[...] + _dot(mixed.astype(BF16), wmix_ref[...])
    h = _swiglu_half_step(h, g2_ref[...], wg_ref, wu_ref, wd_ref)
    gate = _sigmoid(_dot(_rms(h, gple_ref[...]).astype(BF16), wpg_ref[...]))
    h = h + gate * _dot(p_ref[...].astype(BF16), wpp_ref[...])
    y_ref[...] = _rms(h, gfin_ref[...]) if final_norm else h


def _tail(h, gyc, gb, os_, ls_, p, wao, wmix, g2, wg, wu, wd, gple, wpg, wpp, gfin, tm,
          final_norm):
    t, d = h.shape
    row = lambda w: pl.BlockSpec((tm, w), lambda i: (i, 0))
    strided = lambda dil, w: pl.BlockSpec((tm // dil, dil * w), lambda i: (i, 0))
    res = lambda x: _resident(x.shape)
    weights = (wao, wmix, g2, wg, wu, wd, gple, wpg, wpp, gfin)
    dils = [dil for _, dil in GROUPS]
    return pl.pallas_call(
        functools.partial(_tail_kernel, tm=tm, final_norm=final_norm),
        grid=(t // tm,),
        in_specs=[row(d), row(d), row(d)] + [strided(dil, GROUP_W) for dil in dils]
                 + [strided(dil, LANES) for dil in dils]
                 + [row(p.shape[1])] + [res(x) for x in weights],
        out_specs=row(d),
        out_shape=jax.ShapeDtypeStruct((t, d), F32),
        scratch_shapes=[pltpu.VMEM((N_GROUPS, GROUP_W // LANES, tm, LANES), F32),
                        pltpu.VMEM((N_GROUPS, 1, tm, LANES), F32)],
        compiler_params=_cparams(1),
        name="tail",
    )(h, gyc, gb, *os_, *ls_, p, *weights)


def kernel(x, p, positions, g_ffn1, w_ffn1_gate, w_ffn1_up, w_ffn1_down, g_mix, w_in,
           w_dw, b_dw, g_conv_ln, b_conv_ln, w_conv_out, w_attn_out, w_mix_out,
           g_ffn2, w_ffn2_gate, w_ffn2_up, w_ffn2_down, g_ple, w_ple_gate, w_ple_proj,
           g_final):
    b, s, d = x.shape
    depth = p.shape[0]
    t = b * s
    c_conv = w_dw.shape[-1]
    assert w_in.shape[-1] == 2 * c_conv + 3 * ATTN_W + 2 * d
    tm, tm_mix = 512, 256
    max_dil = max(dil for _, dil in GROUPS)
    assert s % tm == 0 and s % tm_mix == 0
    assert tm_mix % CONV_ROWS == 0 and (tm_mix // max_dil) % 16 == 0
    for window, dil in GROUPS:
        assert window // dil == Q_BLOCK and (s // dil) % Q_BLOCK == 0

    cos, sin = _rope_tables(positions, tm)
    row = lambda v: v.reshape(1, -1)
    bf = lambda w: w.astype(BF16)

    h = x.reshape(t, d)
    for i in range(depth):
        h, u = _ffn1(h, row(g_ffn1[i]), bf(w_ffn1_gate[i]), bf(w_ffn1_up[i]),
                     bf(w_ffn1_down[i]), row(g_mix[i]), tm)
        gyc, q0, q1, q2, k0, k1, k2, v0, v1, v2, gb = _mixer_in(
            u, bf(w_in[i]), cos, sin, w_dw[i], row(b_dw[i]), row(g_conv_ln[i]),
            row(b_conv_ln[i]), bf(w_conv_out[i]), b, tm_mix)
        os_, ls_ = [], []
        for (_, dil), q, k, v in zip(GROUPS, (q0, q1, q2), (k0, k1, k2), (v0, v1, v2)):
            o, lse = _attention_group(q, k, v, b, dil, 1024)
            os_.append(o)
            ls_.append(lse)
        h = _tail(h, gyc, gb, os_, ls_, p[i].reshape(t, -1),
                  bf(w_attn_out[i]), bf(w_mix_out[i]), row(g_ffn2[i]), bf(w_ffn2_gate[i]),
                  bf(w_ffn2_up[i]), bf(w_ffn2_down[i]), row(g_ple[i]), bf(w_ple_gate[i]),
                  bf(w_ple_proj[i]), row(g_final), tm, i == depth - 1)
    return h.reshape(b, s, d)
```

```python
import functools

import jax
import jax.numpy as jnp
from jax import lax
from jax.experimental import pallas as pl
from jax.experimental.pallas import tpu as pltpu

HEAD_DIM = 128
HEADS_PER_GROUP = 4
GROUPS = ((128, 1), (512, 4), (2048, 16))
N_GROUPS = len(GROUPS)
GROUP_W = HEADS_PER_GROUP * HEAD_DIM
ATTN_W = N_GROUPS * GROUP_W
CONV_WIDTH = 31
Q_BLOCK = 128
ROPE_THETA = 10000.0
EPS = 1e-6
NEG = -1e30

LANES = 128
SUBLANES = 8
CONV_HALO = 32
CONV_ROWS = 64
ATTN_UNROLL = 4
STAT_REP = LANES // HEADS_PER_GROUP
VMEM_LIMIT = 56 * 1024 * 1024

BF16 = jnp.bfloat16
F32 = jnp.float32


def _cparams(n_axes):
    return pltpu.CompilerParams(
        dimension_semantics=("arbitrary",) * n_axes, vmem_limit_bytes=VMEM_LIMIT)


def _resident(shape):
    nd = len(shape)
    return pl.BlockSpec(shape, lambda *_: (0,) * nd, pipeline_mode=pl.Buffered(1))


def _rms(x, g):
    ms = jnp.mean(x * x, axis=-1, keepdims=True)
    return x * lax.rsqrt(ms + EPS) * g


def _sigmoid(x):
    return 1.0 / (1.0 + jnp.exp(-x))


def _dot(a, b):
    return jnp.dot(a, b, preferred_element_type=F32)


def _swiglu_half_step(x, g, wg_ref, wu_ref, wd_ref):
    xn = _rms(x, g).astype(BF16)
    gate = _dot(xn, wg_ref[...])
    up = _dot(xn, wu_ref[...])
    act = (gate * _sigmoid(gate) * up).astype(BF16)
    return x + 0.5 * _dot(act, wd_ref[...])


def _rope_kernel(pos_ref, freq_ref, sign_ref, cos_ref, sin_ref):
    ang = pos_ref[...].astype(F32) * freq_ref[...]
    cos_ref[...] = jnp.cos(ang)
    sin_ref[...] = jnp.sin(ang) * sign_ref[...]


def _rope_tables(positions, tm):
    t = positions.size
    half = HEAD_DIM // 2
    inv_freq = ROPE_THETA ** (-jnp.arange(0, HEAD_DIM, 2, dtype=F32) / HEAD_DIM)
    freq = jnp.concatenate([inv_freq, inv_freq])[None, :]
    sign = jnp.concatenate([-jnp.ones((half,), F32), jnp.ones((half,), F32)])[None, :]
    tile = pl.BlockSpec((tm, HEAD_DIM), lambda i: (i, 0))
    const = pl.BlockSpec((1, HEAD_DIM), lambda i: (0, 0))
    return pl.pallas_call(
        _rope_kernel,
        grid=(t // tm,),
        in_specs=[pl.BlockSpec((tm, 1), lambda i: (i, 0)), const, const],
        out_specs=[tile, tile],
        out_shape=[jax.ShapeDtypeStruct((t, HEAD_DIM), F32)] * 2,
        compiler_params=_cparams(1),
        name="rope_tables",
    )(positions.reshape(t, 1), freq, sign)


def _ffn1_kernel(x_ref, g1_ref, wg_ref, wu_ref, wd_ref, gmix_ref, h_ref, u_ref):
    h = _swiglu_half_step(x_ref[...], g1_ref[...], wg_ref, wu_ref, wd_ref)
    h_ref[...] = h
    u_ref[...] = _rms(h, gmix_ref[...]).astype(BF16)


def _ffn1(x, g1, wg, wu, wd, gmix, tm):
    t, d = x.shape
    ff = wg.shape[1]
    tile = pl.BlockSpec((tm, d), lambda i: (i, 0))
    return pl.pallas_call(
        _ffn1_kernel,
        grid=(t // tm,),
        in_specs=[tile, _resident((1, d)), _resident((d, ff)), _resident((d, ff)),
                  _resident((ff, d)), _resident((1, d))],
        out_specs=[tile, tile],
        out_shape=[jax.ShapeDtypeStruct((t, d), F32), jax.ShapeDtypeStruct((t, d), BF16)],
        compiler_params=_cparams(1),
        name="ffn1",
    )(x, g1, wg, wu, wd, gmix)


def _rotary(t, cos, sin_signed):
    return t * cos + pltpu.roll(t, HEAD_DIM // 2, axis=1) * sin_signed


def _conv_lane_chunk(xbuf, w_ref, b_ref, acc_ref, tm, c):
    off = CONV_HALO - (CONV_WIDTH - 1)
    cols = slice(c * LANES, (c + 1) * LANES)
    taps = [w_ref[j:j + 1, cols] for j in range(CONV_WIDTH)]
    bias = b_ref[:, cols]
    for rb in range(tm // CONV_ROWS):
        base = rb * CONV_ROWS
        out = None
        for r in range(SUBLANES):
            n = CONV_ROWS if r == 0 else CONV_ROWS + SUBLANES
            part = None
            for q in range((off + CONV_WIDTH - 1) // SUBLANES + 1):
                j = SUBLANES * q + r - off
                if 0 <= j < CONV_WIDTH:
                    lo = base + SUBLANES * q
                    term = taps[j] * xbuf[lo:lo + n, cols]
                    part = term if part is None else part + term
            shifted = part if r == 0 else part[r:r + CONV_ROWS]
            out = shifted if out is None else out + shifted
        acc_ref[base:base + CONV_ROWS, cols] = out + bias


def _mixer_in_kernel(u_ref, w_ref, cos_ref, sin_ref, wdw_ref, bdw_ref, gln_ref, bln_ref, wco_ref,
                     gyc_ref, q0_ref, q1_ref, q2_ref, k0_ref, k1_ref, k2_ref,
                     v0_ref, v1_ref, v2_ref, gb_ref, xbuf, acc_ref, ga_buf, stage,
                     *, tm, c_conv, d_model):
    def proj(lo, width):
        return _dot(u_ref[...], w_ref[:, lo:lo + width])

    @pl.when(pl.program_id(1) == 0)
    def _():
        xbuf[0:CONV_HALO, :] = jnp.zeros((CONV_HALO, c_conv), F32)

    a = proj(0, c_conv)
    gate = proj(c_conv, c_conv)
    xbuf[CONV_HALO:, :] = a * _sigmoid(gate)

    def emit(out_ref, val, dil):
        if dil == 1:
            out_ref[...] = val.astype(BF16)
            return
        rows = tm // dil
        for c in range(GROUP_W // LANES):
            stage[c] = val[:, c * LANES:(c + 1) * LANES]
        for r in range(dil):
            for c in range(GROUP_W // LANES):
                lo = r * GROUP_W + c * LANES
                out_ref[:, lo:lo + LANES] = stage[c, pl.ds(r, rows, stride=dil), :].astype(BF16)

    base = 2 * c_conv
    gbase = base + 3 * ATTN_W

    def rotary_piece(out_ref, lo, dil, scale):
        def piece():
            cos, sin = cos_ref[...], sin_ref[...]
            t = proj(lo, GROUP_W)
            heads = [_rotary(t[:, h * HEAD_DIM:(h + 1) * HEAD_DIM], cos, sin)
                     for h in range(HEADS_PER_GROUP)]
            if scale is not None:
                heads = [x * scale for x in heads]
            emit(out_ref, jnp.concatenate(heads, axis=1), dil)
        return piece

    def value_piece(out_ref, lo, dil):
        return lambda: emit(out_ref, proj(lo, GROUP_W), dil)

    def gate_a_piece():
        ga_buf[...] = _sigmoid(proj(gbase, d_model))

    def gate_b_piece():
        gb_ref[...] = _sigmoid(proj(gbase + d_model, d_model)).astype(BF16)

    pieces = [gate_a_piece, gate_b_piece]
    for g, (q_ref, k_ref, v_ref) in enumerate(
            ((q0_ref, k0_ref, v0_ref), (q1_ref, k1_ref, v1_ref), (q2_ref, k2_ref, v2_ref))):
        dil = GROUPS[g][1]
        pieces.append(rotary_piece(q_ref, base + g * GROUP_W, dil, HEAD_DIM ** -0.5))
        pieces.append(rotary_piece(k_ref, base + ATTN_W + g * GROUP_W, dil, None))
        pieces.append(value_piece(v_ref, base + 2 * ATTN_W + g * GROUP_W, dil))

    for c in range(c_conv // LANES):
        _conv_lane_chunk(xbuf, wdw_ref, bdw_ref, acc_ref, tm, c)
    xbuf[0:CONV_HALO, :] = xbuf[tm:tm + CONV_HALO, :]
    for piece in pieces:
        piece()

    acc = acc_ref[...]
    mu = jnp.mean(acc, axis=-1, keepdims=True)
    cen = acc - mu
    var = jnp.mean(cen * cen, axis=-1, keepdims=True)
    y = cen * lax.rsqrt(var + EPS) * gln_ref[...] + bln_ref[...]
    y = (y * _sigmoid(y)).astype(BF16)
    gyc_ref[...] = (ga_buf[...] * _dot(y, wco_ref[...])).astype(BF16)


def _mixer_in(u, w_in, cos, sin, w_dw, b_dw, g_ln, b_ln, w_co, batch, tm):
    t, d = u.shape
    in_w = w_in.shape[1]
    c_conv = w_dw.shape[1]
    per_seq = t // batch // tm
    idx = lambda b, i: (b * per_seq + i, 0)
    row = lambda w: pl.BlockSpec((tm, w), idx)
    strided = lambda dil: pl.BlockSpec((tm // dil, dil * GROUP_W), idx)
    strided_sds = lambda dil: jax.ShapeDtypeStruct((t // dil, dil * GROUP_W), BF16)
    dils = [dil for _, dil in GROUPS]
    out_specs = [row(d)] + [strided(dil) for _ in range(3) for dil in dils] + [row(d)]
    out_shape = ([jax.ShapeDtypeStruct((t, d), BF16)]
                 + [strided_sds(dil) for _ in range(3) for dil in dils]
                 + [jax.ShapeDtypeStruct((t, d), BF16)])
    return pl.pallas_call(
        functools.partial(_mixer_in_kernel, tm=tm, c_conv=c_conv, d_model=d),
        grid=(batch, per_seq),
        in_specs=[row(d), _resident((d, in_w)), row(HEAD_DIM), row(HEAD_DIM),
                  _resident((CONV_WIDTH, c_conv)), _resident((1, c_conv)), _resident((1, c_conv)),
                  _resident((1, c_conv)), _resident((c_conv, d))],
        out_specs=out_specs,
        out_shape=out_shape,
        scratch_shapes=[pltpu.VMEM((tm + CONV_HALO, c_conv), F32),
                        pltpu.VMEM((tm, c_conv), F32),
                        pltpu.VMEM((tm, d), F32),
                        pltpu.VMEM((GROUP_W // LANES, tm, LANES), F32)],
        compiler_params=_cparams(2),
        name="mixer_in",
    )(u, w_in, cos, sin, w_dw, b_dw, g_ln, b_ln, w_co)


def _attn_kernel(q_ref, k_ref, v_ref, kh_ref, vh_ref, o_ref, lse_ref, kbuf, vbuf, *, lt, unroll):
    first_tile = pl.program_id(2) == 0
    kbuf[0:Q_BLOCK, :] = kh_ref[0]
    kbuf[Q_BLOCK:, :] = k_ref[0]
    ones = jnp.ones((lt + Q_BLOCK, HEAD_DIM), BF16)
    for h in range(HEADS_PER_GROUP):
        sl = slice(h * HEAD_DIM, (h + 1) * HEAD_DIM)
        vbuf[0:Q_BLOCK, 2 * h * HEAD_DIM:(2 * h + 1) * HEAD_DIM] = vh_ref[0, :, sl]
        vbuf[Q_BLOCK:, 2 * h * HEAD_DIM:(2 * h + 1) * HEAD_DIM] = v_ref[0, :, sl]
        vbuf[:, (2 * h + 1) * HEAD_DIM:(2 * h + 2) * HEAD_DIM] = ones

    qi = lax.broadcasted_iota(jnp.int32, (Q_BLOCK, 2 * Q_BLOCK), 0)
    ki = lax.broadcasted_iota(jnp.int32, (Q_BLOCK, 2 * Q_BLOCK), 1)
    dist = qi + Q_BLOCK - ki
    band = (dist >= 0) & (dist <= Q_BLOCK)
    lane = lax.broadcasted_iota(jnp.int32, (Q_BLOCK, LANES), 1)
    nt = (((1,), (1,)), ((), ()))

    def blocks(units):
        work = [(start, mask, h) for start, mask in units for h in range(HEADS_PER_GROUP)]
        col = lambda h: slice(h * HEAD_DIM, (h + 1) * HEAD_DIM)
        s = [lax.dot_general(q_ref[0, pl.ds(start, Q_BLOCK), col(h)],
                             kbuf[pl.ds(start, 2 * Q_BLOCK), col(h)], nt,
                             preferred_element_type=F32) for start, _, h in work]
        s = [jnp.where(mask, x, NEG) for x, (_, mask, _) in zip(s, work)]
        m = [jnp.max(x, axis=-1, keepdims=True) for x in s]
        e = [jnp.exp(x - mx).astype(BF16) for x, mx in zip(s, m)]
        pv = [_dot(ex, vbuf[pl.ds(start, 2 * Q_BLOCK), 2 * h * HEAD_DIM:(2 * h + 2) * HEAD_DIM])
              for ex, (start, _, h) in zip(e, work)]
        for u, (start, _) in enumerate(units):
            rows = pl.ds(start, Q_BLOCK)
            packed = None
            for h in range(HEADS_PER_GROUP):
                i = u * HEADS_PER_GROUP + h
                den = pv[i][:, HEAD_DIM:]
                o_ref[0, rows, col(h)] = (pv[i][:, :HEAD_DIM] / den).astype(BF16)
                lse = m[i] + jnp.log(den)
                packed = lse if packed is None else jnp.where(lane >= h * STAT_REP, lse, packed)
            lse_ref[0, rows, :] = packed

    first_mask = band & jnp.logical_or(ki >= Q_BLOCK, jnp.logical_not(first_tile))
    blocks([(0, first_mask)] + [(j * Q_BLOCK, band) for j in range(1, unroll)])

    def body(n, carry):
        start = n * (unroll * Q_BLOCK)
        blocks([(pl.multiple_of(start + j * Q_BLOCK, Q_BLOCK), band) for j in range(unroll)])
        return carry

    lax.fori_loop(1, lt // (unroll * Q_BLOCK), body, 0)


def _attention_group(q, k, v, batch, dilation, lt):
    rows, _ = q.shape
    l = rows // batch
    lt = min(lt, l)
    per = lt // Q_BLOCK
    unroll = min(ATTN_UNROLL, per)
    assert l % lt == 0 and per % unroll == 0
    view = lambda x: x.reshape(batch, l, x.shape[-1])
    tile = lambda width: pl.BlockSpec((1, lt, width), lambda bi, r, i: (bi, i, r))
    halo = pl.BlockSpec((1, Q_BLOCK, GROUP_W), lambda bi, r, i: (bi, jnp.maximum(i * per - 1, 0), r))
    o, lse = pl.pallas_call(
        functools.partial(_attn_kernel, lt=lt, unroll=unroll),
        grid=(batch, dilation, l // lt),
        in_specs=[tile(GROUP_W), tile(GROUP_W), tile(GROUP_W), halo, halo],
        out_specs=[tile(GROUP_W), tile(LANES)],
        out_shape=[jax.ShapeDtypeStruct((batch, l, dilation * GROUP_W), BF16),
                   jax.ShapeDtypeStruct((batch, l, dilation * LANES), F32)],
        scratch_shapes=[pltpu.VMEM((lt + Q_BLOCK, GROUP_W), BF16),
                        pltpu.VMEM((lt + Q_BLOCK, 2 * GROUP_W), BF16)],
        compiler_params=_cparams(3),
        name=f"attn_d{dilation}",
    )(view(q), view(k), view(v), view(k), view(v))
    return o.reshape(rows, dilation * GROUP_W), lse.reshape(rows, dilation * LANES)


def _tail_kernel(h_ref, gyc_ref, gb_ref, o0_ref, o1_ref, o2_ref, l0_ref, l1_ref, l2_ref, p_ref,
                 wao_ref, wmix_ref, g2_ref, wg_ref, wu_ref, wd_ref, gple_ref, wpg_ref, wpp_ref,
                 gfin_ref, y_ref, obuf, lbuf, *, tm, final_norm):
    def token_order(dst, src_ref, dil, width):
        chunks = width // LANES
        if dil == 1:
            return [src_ref[:, c * LANES:(c + 1) * LANES].astype(F32) for c in range(chunks)]
        rows = tm // dil
        for r in range(dil):
            for c in range(chunks):
                lo = r * width + c * LANES
                dst[c, pl.ds(r, rows, stride=dil), :] = src_ref[:, lo:lo + LANES].astype(F32)
        return [dst[c] for c in range(chunks)]

    dils = [dil for _, dil in GROUPS]
    ls = [token_order(lbuf.at[g], ref, dils[g], LANES)[0]
          for g, ref in enumerate((l0_ref, l1_ref, l2_ref))]
    os_ = [token_order(obuf.at[g], ref, dils[g], GROUP_W)
           for g, ref in enumerate((o0_ref, o1_ref, o2_ref))]

    m = jnp.maximum(jnp.maximum(ls[0], ls[1]), ls[2])
    es = [jnp.exp(x - m) for x in ls]
    inv = 1.0 / (es[0] + es[1] + es[2])
    alphas = [x * inv for x in es]
    src = lax.broadcasted_iota(jnp.int32, (2 * LANES, GROUP_W), 0) % LANES
    dst_head = lax.broadcasted_iota(jnp.int32, (2 * LANES, GROUP_W), 1) // HEAD_DIM
    spread = jnp.where(src == dst_head * STAT_REP, 1.0, 0.0).astype(BF16)
    wide = []
    for a in alphas:
        hi = a.astype(BF16)
        lo = (a - hi.astype(F32)).astype(BF16)
        wide.append(_dot(jnp.concatenate([hi, lo], axis=1), spread))
    heads = []
    for h in range(HEADS_PER_GROUP):
        acc = None
        for g in range(N_GROUPS):
            term = wide[g][:, h * HEAD_DIM:(h + 1) * HEAD_DIM] * os_[g][h]
            acc = term if acc is None else acc + term
        heads.append(acc.astype(BF16))
    o = jnp.concatenate(heads, axis=1)
    y_attn = _dot(o, wao_ref[...])
    mixed = gyc_ref[...].astype(F32) + gb_ref[...].astype(F32) * y_attn
    h = h_ref[...] + _dot(mixed.astype(BF16), wmix_ref[...])
    h = _swiglu_half_step(h, g2_ref[...], wg_ref, wu_ref, wd_ref)
    gate = _sigmoid(_dot(_rms(h, gple_ref[...]).astype(BF16), wpg_ref[...]))
    h = h + gate * _dot(p_ref[...].astype(BF16), wpp_ref[...])
    y_ref[...] = _rms(h, gfin_ref[...]) if final_norm else h


def _tail(h, gyc, gb, os_, ls_, p, wao, wmix, g2, wg, wu, wd, gple, wpg, wpp, gfin, tm,
          final_norm):
    t, d = h.shape
    row = lambda w: pl.BlockSpec((tm, w), lambda i: (i, 0))
    strided = lambda dil, w: pl.BlockSpec((tm // dil, dil * w), lambda i: (i, 0))
    res = lambda x: _resident(x.shape)
    weights = (wao, wmix, g2, wg, wu, wd, gple, wpg, wpp, gfin)
    dils = [dil for _, dil in GROUPS]
    return pl.pallas_call(
        functools.partial(_tail_kernel, tm=tm, final_norm=final_norm),
        grid=(t // tm,),
        in_specs=[row(d), row(d), row(d)] + [strided(dil, GROUP_W) for dil in dils]
                 + [strided(dil, LANES) for dil in dils]
                 + [row(p.shape[1])] + [res(x) for x in weights],
        out_specs=row(d),
        out_shape=jax.ShapeDtypeStruct((t, d), F32),
        scratch_shapes=[pltpu.VMEM((N_GROUPS, GROUP_W // LANES, tm, LANES), F32),
                        pltpu.VMEM((N_GROUPS, 1, tm, LANES), F32)],
        compiler_params=_cparams(1),
        name="tail",
    )(h, gyc, gb, *os_, *ls_, p, *weights)


def kernel(x, p, positions, g_ffn1, w_ffn1_gate, w_ffn1_up, w_ffn1_down, g_mix, w_in,
           w_dw, b_dw, g_conv_ln, b_conv_ln, w_conv_out, w_attn_out, w_mix_out,
           g_ffn2, w_ffn2_gate, w_ffn2_up, w_ffn2_down, g_ple, w_ple_gate, w_ple_proj,
           g_final):
    b, s, d = x.shape
    depth = p.shape[0]
    t = b * s
    c_conv = w_dw.shape[-1]
    assert w_in.shape[-1] == 2 * c_conv + 3 * ATTN_W + 2 * d
    tm, tm_mix = 512, 256
    max_dil = max(dil for _, dil in GROUPS)
    assert s % tm == 0 and s % tm_mix == 0
    assert tm_mix % CONV_ROWS == 0 and (tm_mix // max_dil) % 16 == 0
    for window, dil in GROUPS:
        assert window // dil == Q_BLOCK and (s // dil) % Q_BLOCK == 0

    cos, sin = _rope_tables(positions, tm)
    row = lambda v: v.reshape(1, -1)
    bf = lambda w: w.astype(BF16)

    h = x.reshape(t, d)
    for i in range(depth):
        h, u = _ffn1(h, row(g_ffn1[i]), bf(w_ffn1_gate[i]), bf(w_ffn1_up[i]),
                     bf(w_ffn1_down[i]), row(g_mix[i]), tm)
        gyc, q0, q1, q2, k0, k1, k2, v0, v1, v2, gb = _mixer_in(
            u, bf(w_in[i]), cos, sin, w_dw[i], row(b_dw[i]), row(g_conv_ln[i]),
            row(b_conv_ln[i]), bf(w_conv_out[i]), b, tm_mix)
        os_, ls_ = [], []
        for (_, dil), q, k, v in zip(GROUPS, (q0, q1, q2), (k0, k1, k2), (v0, v1, v2)):
            o, lse = _attention_group(q, k, v, b, dil, 1024)
            os_.append(o)
            ls_.append(lse)
        h = _tail(h, gyc, gb, os_, ls_, p[i].reshape(t, -1),
                  bf(w_attn_out[i]), bf(w_mix_out[i]), row(g_ffn2[i]), bf(w_ffn2_gate[i]),
                  bf(w_ffn2_up[i]), bf(w_ffn2_down[i]), row(g_ple[i]), bf(w_ple_gate[i]),
                  bf(w_ple_proj[i]), row(g_final), tm, i == depth - 1)
    return h.reshape(b, s, d)
```

```python
import functools

import jax
import jax.numpy as jnp
from jax import lax
from jax.experimental import pallas as pl
from jax.experimental.pallas import tpu as pltpu

HEAD_DIM = 128
HEADS_PER_GROUP = 4
GROUPS = ((128, 1), (512, 4), (2048, 16))
N_GROUPS = len(GROUPS)
GROUP_W = HEADS_PER_GROUP * HEAD_DIM
ATTN_W = N_GROUPS * GROUP_W
CONV_WIDTH = 31
Q_BLOCK = 128
ROPE_THETA = 10000.0
EPS = 1e-6
NEG = -1e30

LANES = 128
SUBLANES = 8
MXU_COLS = 256
CONV_HALO = 32
CONV_ROWS = 64
ATTN_UNROLL = 4
ATTN_ROWS = 1024
STAT_REP = LANES // HEADS_PER_GROUP
VMEM_LIMIT = 56 * 1024 * 1024

BF16 = jnp.bfloat16
F32 = jnp.float32


def _cparams(n_axes):
    return pltpu.CompilerParams(
        dimension_semantics=("arbitrary",) * n_axes, vmem_limit_bytes=VMEM_LIMIT)


def _resident(shape):
    nd = len(shape)
    return pl.BlockSpec(shape, lambda *_: (0,) * nd, pipeline_mode=pl.Buffered(1))


def _rms(x, g):
    ms = jnp.mean(x * x, axis=-1, keepdims=True)
    return x * lax.rsqrt(ms + EPS) * g


def _sigmoid(x):
    return 1.0 / (1.0 + jnp.exp(-x))


def _dot(a, b):
    return jnp.dot(a, b, preferred_element_type=F32)


def _swiglu_half_step(x, g, wg_ref, wu_ref, wd_ref):
    xn = _rms(x, g).astype(BF16)
    ff = wg_ref.shape[1]
    acts = []
    for lo in range(0, ff, MXU_COLS):
        gate = _dot(xn, wg_ref[:, lo:lo + MXU_COLS])
        up = _dot(xn, wu_ref[:, lo:lo + MXU_COLS])
        acts.append((gate * _sigmoid(gate) * up).astype(BF16))
    act = jnp.concatenate(acts, axis=1)
    return x + 0.5 * _dot(act, wd_ref[...])


def _rope_kernel(pos_lo_ref, pos_hi_ref, freq_ref, cos_ref, sin_ref):
    lane = lax.broadcasted_iota(jnp.int32, cos_ref.shape, 1)
    pos = jnp.where(lane < HEAD_DIM // 2, pos_lo_ref[...], pos_hi_ref[...])
    ang = pos.astype(F32) * freq_ref[...]
    cos_ref[...] = jnp.cos(ang)
    sin_ref[...] = jnp.sin(ang)


def _rope_tables(positions, tm):
    t = positions.size
    half_t = t // 2
    inv_freq = ROPE_THETA ** (-jnp.arange(0, HEAD_DIM, 2, dtype=F32) / HEAD_DIM)
    freq = jnp.concatenate([inv_freq, inv_freq])[None, :]
    pos = positions.reshape(2, half_t, 1)
    tile = pl.BlockSpec((tm, HEAD_DIM), lambda i: (i, 0))
    col = pl.BlockSpec((tm, 1), lambda i: (i, 0))
    return pl.pallas_call(
        _rope_kernel,
        grid=(half_t // tm,),
        in_specs=[col, col, pl.BlockSpec((1, HEAD_DIM), lambda i: (0, 0))],
        out_specs=[tile, tile],
        out_shape=[jax.ShapeDtypeStruct((half_t, HEAD_DIM), F32)] * 2,
        compiler_params=_cparams(1),
        name="rope_tables",
    )(pos[0], pos[1], freq)


def _unpack_rope(cos_ref, sin_ref, upper_half):
    half = HEAD_DIM // 2
    lane = lax.broadcasted_iota(jnp.int32, cos_ref.shape, 1)
    low = lane < half
    own = jnp.logical_xor(low, upper_half)
    c, s = cos_ref[...], sin_ref[...]
    cos = jnp.where(own, c, pltpu.roll(c, half, axis=1))
    sin = jnp.where(own, s, pltpu.roll(s, half, axis=1))
    return cos, jnp.where(low, -sin, sin)


def _ffn1_kernel(x_ref, g1_ref, wg_ref, wu_ref, wd_ref, gmix_ref, h_ref, u_ref):
    h = _swiglu_half_step(x_ref[...], g1_ref[...], wg_ref, wu_ref, wd_ref)
    h_ref[...] = h
    u_ref[...] = _rms(h, gmix_ref[...]).astype(BF16)


def _ffn1(x, g1, wg, wu, wd, gmix, tm):
    t, d = x.shape
    ff = wg.shape[1]
    tile = pl.BlockSpec((tm, d), lambda i: (i, 0))
    return pl.pallas_call(
        _ffn1_kernel,
        grid=(t // tm,),
        in_specs=[tile, _resident((1, d)), _resident((d, ff)), _resident((d, ff)),
                  _resident((ff, d)), _resident((1, d))],
        out_specs=[tile, tile],
        out_shape=[jax.ShapeDtypeStruct((t, d), F32), jax.ShapeDtypeStruct((t, d), BF16)],
        compiler_params=_cparams(1),
        name="ffn1",
    )(x, g1, wg, wu, wd, gmix)


def _rotary(t, cos, sin_signed):
    return t * cos + pltpu.roll(t, HEAD_DIM // 2, axis=1) * sin_signed


def _conv_lane_chunk(xbuf, w_ref, b_ref, acc_ref, tm, c):
    off = CONV_HALO - (CONV_WIDTH - 1)
    cols = slice(c * LANES, (c + 1) * LANES)
    taps = [w_ref[j:j + 1, cols] for j in range(CONV_WIDTH)]
    bias = b_ref[:, cols]
    for rb in range(tm // CONV_ROWS):
        base = rb * CONV_ROWS
        out = None
        for r in range(SUBLANES):
            n = CONV_ROWS if r == 0 else CONV_ROWS + SUBLANES
            part = None
            for q in range((off + CONV_WIDTH - 1) // SUBLANES + 1):
                j = SUBLANES * q + r - off
                if 0 <= j < CONV_WIDTH:
                    lo = base + SUBLANES * q
                    term = taps[j] * xbuf[lo:lo + n, cols]
                    part = term if part is None else part + term
            shifted = part if r == 0 else part[r:r + CONV_ROWS]
            out = shifted if out is None else out + shifted
        acc_ref[base:base + CONV_ROWS, cols] = out + bias


def _mixer_in_kernel(u_ref, w_ref, cos_ref, sin_ref, wdw_ref, bdw_ref, gln_ref, bln_ref, wco_ref,
                     gyc_ref, q0_ref, q1_ref, q2_ref, k0_ref, k1_ref, k2_ref,
                     v0_ref, v1_ref, v2_ref, gb_ref, xbuf, acc_ref, ga_buf, stage,
                     *, tm, c_conv, d_model):
    def proj(lo, width):
        return _dot(u_ref[...], w_ref[:, lo:lo + width])

    @pl.when(pl.program_id(1) == 0)
    def _():
        xbuf[0:CONV_HALO, :] = jnp.zeros((CONV_HALO, c_conv), F32)

    a = proj(0, c_conv)
    gate = proj(c_conv, c_conv)
    xbuf[CONV_HALO:, :] = a * _sigmoid(gate)

    def emit(out_ref, val, dil):
        if dil == 1:
            out_ref[...] = val.astype(BF16)
            return
        rows = tm // dil
        for c in range(GROUP_W // LANES):
            stage[c] = val[:, c * LANES:(c + 1) * LANES]
        for r in range(dil):
            for c in range(GROUP_W // LANES):
                lo = r * GROUP_W + c * LANES
                out_ref[:, lo:lo + LANES] = stage[c, pl.ds(r, rows, stride=dil), :].astype(BF16)

    base = 2 * c_conv
    gbase = base + 3 * ATTN_W

    tile_idx = pl.program_id(0) * pl.num_programs(1) + pl.program_id(1)
    n_tiles = pl.num_programs(0) * pl.num_programs(1)
    cos, sin = _unpack_rope(cos_ref, sin_ref, 2 * tile_idx >= n_tiles)

    def rotary_piece(out_ref, lo, dil, scale):
        def piece():
            t = proj(lo, GROUP_W)
            heads = [_rotary(t[:, h * HEAD_DIM:(h + 1) * HEAD_DIM], cos, sin)
                     for h in range(HEADS_PER_GROUP)]
            if scale is not None:
                heads = [x * scale for x in heads]
            emit(out_ref, jnp.concatenate(heads, axis=1), dil)
        return piece

    def value_piece(out_ref, lo, dil):
        return lambda: emit(out_ref, proj(lo, GROUP_W), dil)

    def gate_a_piece():
        ga_buf[...] = _sigmoid(proj(gbase, d_model))

    def gate_b_piece():
        gb_ref[...] = _sigmoid(proj(gbase + d_model, d_model)).astype(BF16)

    pieces = [gate_a_piece, gate_b_piece]
    for g, (q_ref, k_ref, v_ref) in enumerate(
            ((q0_ref, k0_ref, v0_ref), (q1_ref, k1_ref, v1_ref), (q2_ref, k2_ref, v2_ref))):
        dil = GROUPS[g][1]
        pieces.append(rotary_piece(q_ref, base + g * GROUP_W, dil, HEAD_DIM ** -0.5))
        pieces.append(rotary_piece(k_ref, base + ATTN_W + g * GROUP_W, dil, None))
        pieces.append(value_piece(v_ref, base + 2 * ATTN_W + g * GROUP_W, dil))

    for c in range(c_conv // LANES):
        _conv_lane_chunk(xbuf, wdw_ref, bdw_ref, acc_ref, tm, c)
    xbuf[0:CONV_HALO, :] = xbuf[tm:tm + CONV_HALO, :]
    for piece in pieces:
        piece()

    acc = acc_ref[...]
    mu = jnp.mean(acc, axis=-1, keepdims=True)
    cen = acc - mu
    var = jnp.mean(cen * cen, axis=-1, keepdims=True)
    y = cen * lax.rsqrt(var + EPS) * gln_ref[...] + bln_ref[...]
    y = (y * _sigmoid(y)).astype(BF16)
    gyc_ref[...] = (ga_buf[...] * _dot(y, wco_ref[...])).astype(BF16)


def _mixer_in(u, w_in, cos, sin, w_dw, b_dw, g_ln, b_ln, w_co, batch, tm):
    t, d = u.shape
    in_w = w_in.shape[1]
    c_conv = w_dw.shape[1]
    per_seq = t // batch // tm
    idx = lambda b, i: (b * per_seq + i, 0)
    row = lambda w: pl.BlockSpec((tm, w), idx)
    half_tiles = t // 2 // tm
    rope = pl.BlockSpec((tm, HEAD_DIM), lambda b, i: (lax.rem(b * per_seq + i, half_tiles), 0))
    strided = lambda dil: pl.BlockSpec((tm // dil, dil * GROUP_W), idx)
    strided_sds = lambda dil: jax.ShapeDtypeStruct((t // dil, dil * GROUP_W), BF16)
    dils = [dil for _, dil in GROUPS]
    out_specs = [row(d)] + [strided(dil) for _ in range(3) for dil in dils] + [row(d)]
    out_shape = ([jax.ShapeDtypeStruct((t, d), BF16)]
                 + [strided_sds(dil) for _ in range(3) for dil in dils]
                 + [jax.ShapeDtypeStruct((t, d), BF16)])
    return pl.pallas_call(
        functools.partial(_mixer_in_kernel, tm=tm, c_conv=c_conv, d_model=d),
        grid=(batch, per_seq),
        in_specs=[row(d), _resident((d, in_w)), rope, rope,
                  _resident((CONV_WIDTH, c_conv)), _resident((1, c_conv)), _resident((1, c_conv)),
                  _resident((1, c_conv)), _resident((c_conv, d))],
        out_specs=out_specs,
        out_shape=out_shape,
        scratch_shapes=[pltpu.VMEM((tm + CONV_HALO, c_conv), F32),
                        pltpu.VMEM((tm, c_conv), F32),
                        pltpu.VMEM((tm, d), F32),
                        pltpu.VMEM((GROUP_W // LANES, tm, LANES), F32)],
        compiler_params=_cparams(2),
        name="mixer_in",
    )(u, w_in, cos, sin, w_dw, b_dw, g_ln, b_ln, w_co)


def _attn_kernel(q_ref, k_ref, v_ref, kh_ref, vh_ref, o_ref, lse_ref, kbuf, vbuf,
                 *, lt, unroll, rps):
    first_tile = pl.program_id(2) == 0
    ones = jnp.ones((lt + Q_BLOCK, HEAD_DIM), BF16)
    for rr in range(rps):
        kbuf[rr, 0:Q_BLOCK, :] = kh_ref[0, :, rr * GROUP_W:(rr + 1) * GROUP_W]
        kbuf[rr, Q_BLOCK:, :] = k_ref[0, :, rr * GROUP_W:(rr + 1) * GROUP_W]
        for h in range(HEADS_PER_GROUP):
            src = slice(rr * GROUP_W + h * HEAD_DIM, rr * GROUP_W + (h + 1) * HEAD_DIM)
            vbuf[rr, 0:Q_BLOCK, 2 * h * HEAD_DIM:(2 * h + 1) * HEAD_DIM] = vh_ref[0, :, src]
            vbuf[rr, Q_BLOCK:, 2 * h * HEAD_DIM:(2 * h + 1) * HEAD_DIM] = v_ref[0, :, src]
            vbuf[rr, :, (2 * h + 1) * HEAD_DIM:(2 * h + 2) * HEAD_DIM] = ones

    qi = lax.broadcasted_iota(jnp.int32, (Q_BLOCK, 2 * Q_BLOCK), 0)
    ki = lax.broadcasted_iota(jnp.int32, (Q_BLOCK, 2 * Q_BLOCK), 1)
    dist = qi + Q_BLOCK - ki
    band = (dist >= 0) & (dist <= Q_BLOCK)
    lane = lax.broadcasted_iota(jnp.int32, (Q_BLOCK, LANES), 1)
    nt = (((1,), (1,)), ((), ()))

    def blocks(units):
        work = [(rr, start, mask, h) for rr, start, mask in units for h in range(HEADS_PER_GROUP)]
        col = lambda rr, h: slice(rr * GROUP_W + h * HEAD_DIM, rr * GROUP_W + (h + 1) * HEAD_DIM)
        s = [lax.dot_general(q_ref[0, pl.ds(start, Q_BLOCK), col(rr, h)],
                             kbuf[rr, pl.ds(start, 2 * Q_BLOCK), col(0, h)], nt,
                             preferred_element_type=F32) for rr, start, _, h in work]
        s = [jnp.where(mask, x, NEG) for x, (_, _, mask, _) in zip(s, work)]
        m = [jnp.max(x, axis=-1, keepdims=True) for x in s]
        e = [jnp.exp(x - mx).astype(BF16) for x, mx in zip(s, m)]
        pv = [_dot(ex, vbuf[rr, pl.ds(start, 2 * Q_BLOCK), 2 * h * HEAD_DIM:(2 * h + 2) * HEAD_DIM])
              for ex, (rr, start, _, h) in zip(e, work)]
        for u, (rr, start, _) in enumerate(units):
            rows = pl.ds(start, Q_BLOCK)
            packed = None
            for h in range(HEADS_PER_GROUP):
                i = u * HEADS_PER_GROUP + h
                den = pv[i][:, HEAD_DIM:]
                o_ref[0, rows, col(rr, h)] = (pv[i][:, :HEAD_DIM] / den).astype(BF16)
                lse = m[i] + jnp.log(den)
                packed = lse if packed is None else jnp.where(lane >= h * STAT_REP, lse, packed)
            lse_ref[0, rows, rr * LANES:(rr + 1) * LANES] = packed

    def run(units):
        for lo in range(0, len(units), unroll):
            blocks(units[lo:lo + unroll])

    per = lt // Q_BLOCK
    upr = min(unroll, per)
    first_mask = band & jnp.logical_or(ki >= Q_BLOCK, jnp.logical_not(first_tile))
    run([(rr, j * Q_BLOCK, first_mask if j == 0 else band)
         for rr in range(rps) for j in range(upr)])

    def body(n, carry):
        start = n * (upr * Q_BLOCK)
        run([(rr, pl.multiple_of(start + j * Q_BLOCK, Q_BLOCK), band)
             for rr in range(rps) for j in range(upr)])
        return carry

    lax.fori_loop(1, per // upr, body, 0)


def _attention_group(q, k, v, batch, dilation, lt, rps):
    rows, _ = q.shape
    l = rows // batch
    lt = min(lt, l)
    per = lt // Q_BLOCK
    unroll = min(ATTN_UNROLL, per * rps)
    assert l % lt == 0 and dilation % rps == 0 and per % min(unroll, per) == 0
    view = lambda x: x.reshape(batch, l, x.shape[-1])
    tile = lambda width: pl.BlockSpec((1, lt, rps * width), lambda bi, r, i: (bi, i, r))
    halo = pl.BlockSpec((1, Q_BLOCK, rps * GROUP_W),
                        lambda bi, r, i: (bi, jnp.maximum(i * per - 1, 0), r))
    o, lse = pl.pallas_call(
        functools.partial(_attn_kernel, lt=lt, unroll=unroll, rps=rps),
        grid=(batch, dilation // rps, l // lt),
        in_specs=[tile(GROUP_W), tile(GROUP_W), tile(GROUP_W), halo, halo],
        out_specs=[tile(GROUP_W), tile(LANES)],
        out_shape=[jax.ShapeDtypeStruct((batch, l, dilation * GROUP_W), BF16),
                   jax.ShapeDtypeStruct((batch, l, dilation * LANES), F32)],
        scratch_shapes=[pltpu.VMEM((rps, lt + Q_BLOCK, GROUP_W), BF16),
                        pltpu.VMEM((rps, lt + Q_BLOCK, 2 * GROUP_W), BF16)],
        compiler_params=_cparams(3),
        name=f"attn_d{dilation}",
    )(view(q), view(k), view(v), view(k), view(v))
    return o.reshape(rows, dilation * GROUP_W), lse.reshape(rows, dilation * LANES)


def _tail_kernel(h_ref, gyc_ref, gb_ref, o0_ref, o1_ref, o2_ref, l0_ref, l1_ref, l2_ref, p_ref,
                 wao_ref, wmix_ref, g2_ref, wg_ref, wu_ref, wd_ref, gple_ref, wpg_ref, wpp_ref,
                 gfin_ref, y_ref, obuf, lbuf, *, tm, final_norm):
    def token_order(dst, src_ref, dil, width):
        chunks = width // LANES
        if dil == 1:
            return [src_ref[:, c * LANES:(c + 1) * LANES].astype(F32) for c in range(chunks)]
        rows = tm // dil
        for r in range(dil):
            for c in range(chunks):
                lo = r * width + c * LANES
                dst[c, pl.ds(r, rows, stride=dil), :] = src_ref[:, lo:lo + LANES].astype(F32)
        return [dst[c] for c in range(chunks)]

    dils = [dil for _, dil in GROUPS]
    ls = [token_order(lbuf.at[g], ref, dils[g], LANES)[0]
          for g, ref in enumerate((l0_ref, l1_ref, l2_ref))]
    os_ = [token_order(obuf.at[g], ref, dils[g], GROUP_W)
           for g, ref in enumerate((o0_ref, o1_ref, o2_ref))]

    m = jnp.maximum(jnp.maximum(ls[0], ls[1]), ls[2])
    es = [jnp.exp(x - m) for x in ls]
    inv = 1.0 / (es[0] + es[1] + es[2])
    alphas = [x * inv for x in es]
    src = lax.broadcasted_iota(jnp.int32, (2 * LANES, GROUP_W), 0) % LANES
    dst_head = lax.broadcasted_iota(jnp.int32, (2 * LANES, GROUP_W), 1) // HEAD_DIM
    spread = jnp.where(src == dst_head * STAT_REP, 1.0, 0.0).astype(BF16)
    wide = []
    for a in alphas:
        hi = a.astype(BF16)
        lo = (a - hi.astype(F32)).astype(BF16)
        wide.append(_dot(jnp.concatenate([hi, lo], axis=1), spread))
    heads = []
    for h in range(HEADS_PER_GROUP):
        acc = None
        for g in range(N_GROUPS):
            term = wide[g][:, h * HEAD_DIM:(h + 1) * HEAD_DIM] * os_[g][h]
            acc = term if acc is None else acc + term
        heads.append(acc.astype(BF16))
    o = jnp.concatenate(heads, axis=1)
    y_attn = _dot(o, wao_ref[...])
    mixed = gyc_ref[...].astype(F32) + gb_ref[...].astype(F32) * y_attn
    h = h_ref[...] + _dot(mixed.astype(BF16), wmix_ref[...])
    h = _swiglu_half_step(h, g2_ref[...], wg_ref, wu_ref, wd_ref)
    gate = _sigmoid(_dot(_rms(h, gple_ref[...]).astype(BF16), wpg_ref[...]))
    h = h + gate * _dot(p_ref[...].astype(BF16), wpp_ref[...])
    y_ref[...] = _rms(h, gfin_ref[...]) if final_norm else h


def _tail(h, gyc, gb, os_, ls_, p, wao, wmix, g2, wg, wu, wd, gple, wpg, wpp, gfin, tm,
          final_norm):
    t, d = h.shape
    row = lambda w: pl.BlockSpec((tm, w), lambda i: (i, 0))
    strided = lambda dil, w: pl.BlockSpec((tm // dil, dil * w), lambda i: (i, 0))
    res = lambda x: _resident(x.shape)
    weights = (wao, wmix, g2, wg, wu, wd, gple, wpg, wpp, gfin)
    dils = [dil for _, dil in GROUPS]
    return pl.pallas_call(
        functools.partial(_tail_kernel, tm=tm, final_norm=final_norm),
        grid=(t // tm,),
        in_specs=[row(d), row(d), row(d)] + [strided(dil, GROUP_W) for dil in dils]
                 + [strided(dil, LANES) for dil in dils]
                 + [row(p.shape[1])] + [res(x) for x in weights],
        out_specs=row(d),
        out_shape=jax.ShapeDtypeStruct((t, d), F32),
        scratch_shapes=[pltpu.VMEM((N_GROUPS, GROUP_W // LANES, tm, LANES), F32),
                        pltpu.VMEM((N_GROUPS, 1, tm, LANES), F32)],
        compiler_params=_cparams(1),
        name="tail",
    )(h, gyc, gb, *os_, *ls_, p, *weights)


def kernel(x, p, positions, g_ffn1, w_ffn1_gate, w_ffn1_up, w_ffn1_down, g_mix, w_in,
           w_dw, b_dw, g_conv_ln, b_conv_ln, w_conv_out, w_attn_out, w_mix_out,
           g_ffn2, w_ffn2_gate, w_ffn2_up, w_ffn2_down, g_ple, w_ple_gate, w_ple_proj,
           g_final):
    b, s, d = x.shape
    depth = p.shape[0]
    t = b * s
    c_conv = w_dw.shape[-1]
    assert w_in.shape[-1] == 2 * c_conv + 3 * ATTN_W + 2 * d
    tm, tm_mix = 512, 256
    max_dil = max(dil for _, dil in GROUPS)
    assert s % tm == 0 and s % tm_mix == 0
    assert tm_mix % CONV_ROWS == 0 and (tm_mix // max_dil) % 16 == 0
    for window, dil in GROUPS:
        assert window // dil == Q_BLOCK and (s // dil) % Q_BLOCK == 0

    cos, sin = _rope_tables(positions, tm)
    row = lambda v: v.reshape(1, -1)
    bf = lambda w: w.astype(BF16)

    h = x.reshape(t, d)
    for i in range(depth):
        h, u = _ffn1(h, row(g_ffn1[i]), bf(w_ffn1_gate[i]), bf(w_ffn1_up[i]),
                     bf(w_ffn1_down[i]), row(g_mix[i]), tm)
        gyc, q0, q1, q2, k0, k1, k2, v0, v1, v2, gb = _mixer_in(
            u, bf(w_in[i]), cos, sin, w_dw[i], row(b_dw[i]), row(g_conv_ln[i]),
            row(b_conv_ln[i]), bf(w_conv_out[i]), b, tm_mix)
        os_, ls_ = [], []
        for (_, dil), q, k, v in zip(GROUPS, (q0, q1, q2), (k0, k1, k2), (v0, v1, v2)):
            o, lse = _attention_group(q, k, v, b, dil, ATTN_ROWS, max(1, ATTN_ROWS * dil // s))
            os_.append(o)
            ls_.append(lse)
        h = _tail(h, gyc, gb, os_, ls_, p[i].reshape(t, -1),
                  bf(w_attn_out[i]), bf(w_mix_out[i]), row(g_ffn2[i]), bf(w_ffn2_gate[i]),
                  bf(w_ffn2_up[i]), bf(w_ffn2_down[i]), row(g_ple[i]), bf(w_ple_gate[i]),
                  bf(w_ple_proj[i]), row(g_final), tm, i == depth - 1)
    return h.reshape(b, s, d)
```

```python
import functools

import jax
import jax.numpy as jnp
from jax import lax
from jax.experimental import pallas as pl
from jax.experimental.pallas import tpu as pltpu

HEAD_DIM = 128
HEADS_PER_GROUP = 4
GROUPS = ((128, 1), (512, 4), (2048, 16))
N_GROUPS = len(GROUPS)
GROUP_W = HEADS_PER_GROUP * HEAD_DIM
ATTN_W = N_GROUPS * GROUP_W
CONV_WIDTH = 31
Q_BLOCK = 128
ROPE_THETA = 10000.0
EPS = 1e-6
NEG = -1e30

LANES = 128
SUBLANES = 8
BF16_ROWS = 16
MXU_COLS = 256
CONV_HALO = 32
CONV_ROWS = 128
ATTN_UNROLL = 4
ATTN_ROWS = 1024
FFN1_PARTS = 2
TAIL_PARTS = 1
STAT_REP = LANES // HEADS_PER_GROUP
VMEM_LIMIT = 56 * 1024 * 1024

BF16 = jnp.bfloat16
F32 = jnp.float32


def _cparams(n_axes):
    return pltpu.CompilerParams(
        dimension_semantics=("arbitrary",) * n_axes, vmem_limit_bytes=VMEM_LIMIT)


def _resident(shape):
    nd = len(shape)
    return pl.BlockSpec(shape, lambda *_: (0,) * nd, pipeline_mode=pl.Buffered(1))


def _rms(x, g):
    ms = jnp.mean(x * x, axis=-1, keepdims=True)
    return x * lax.rsqrt(ms + EPS) * g


def _sigmoid(x):
    return 1.0 / (1.0 + jnp.exp(-x))


def _dot(a, b):
    return jnp.dot(a, b, preferred_element_type=F32)


def _swiglu_half_step(x, g, wg_ref, wu_ref, wd_ref):
    xn = _rms(x, g).astype(BF16)
    ff = wg_ref.shape[1]
    acts = []
    for lo in range(0, ff, MXU_COLS):
        gate = _dot(xn, wg_ref[:, lo:lo + MXU_COLS])
        up = _dot(xn, wu_ref[:, lo:lo + MXU_COLS])
        acts.append((gate * _sigmoid(gate) * up).astype(BF16))
    act = jnp.concatenate(acts, axis=1)
    return x + 0.5 * _dot(act, wd_ref[...])


def _cast_plan(weights, steps):
    in_specs, out_specs, out_shape = [], [], []
    for w in weights:
        rows, cols = w.shape
        slab = rows // steps
        assert rows % steps == 0 and slab % BF16_ROWS == 0, (w.shape, steps)
        spec = pl.BlockSpec((slab, cols), lambda i: (i, 0))
        in_specs.append(spec)
        out_specs.append(spec)
        out_shape.append(jax.ShapeDtypeStruct((rows, cols), BF16))
    return in_specs, out_specs, out_shape


def _cast_slabs(src_refs, dst_refs):
    for src, dst in zip(src_refs, dst_refs):
        dst[...] = src[...].astype(BF16)


def _rope_kernel(pos_lo_ref, pos_hi_ref, freq_ref, *refs):
    n_cast = (len(refs) - 2) // 2
    cos_ref, sin_ref = refs[n_cast:n_cast + 2]
    lane = lax.broadcasted_iota(jnp.int32, cos_ref.shape, 1)
    pos = jnp.where(lane < HEAD_DIM // 2, pos_lo_ref[...], pos_hi_ref[...])
    ang = pos.astype(F32) * freq_ref[...]
    cos_ref[...] = jnp.cos(ang)
    sin_ref[...] = jnp.sin(ang)
    _cast_slabs(refs[:n_cast], refs[n_cast + 2:])


def _rope_tables(positions, tm, cast_weights):
    t = positions.size
    half_t = t // 2
    inv_freq = ROPE_THETA ** (-jnp.arange(0, HEAD_DIM, 2, dtype=F32) / HEAD_DIM)
    freq = jnp.concatenate([inv_freq, inv_freq])[None, :]
    pos = positions.reshape(2, half_t, 1)
    tile = pl.BlockSpec((tm, HEAD_DIM), lambda i: (i, 0))
    col = pl.BlockSpec((tm, 1), lambda i: (i, 0))
    steps = half_t // tm
    c_in, c_out, c_shape = _cast_plan(cast_weights, steps)
    outs = pl.pallas_call(
        _rope_kernel,
        grid=(steps,),
        in_specs=[col, col, pl.BlockSpec((1, HEAD_DIM), lambda i: (0, 0))] + c_in,
        out_specs=[tile, tile] + c_out,
        out_shape=[jax.ShapeDtypeStruct((half_t, HEAD_DIM), F32)] * 2 + c_shape,
        compiler_params=_cparams(1),
        name="rope_tables",
    )(pos[0], pos[1], freq, *cast_weights)
    return outs[0], outs[1], outs[2:]


def _unpack_rope(cos_ref, sin_ref, upper_half):
    half = HEAD_DIM // 2
    lane = lax.broadcasted_iota(jnp.int32, cos_ref.shape, 1)
    low = lane < half
    own = jnp.logical_xor(low, upper_half)
    c, s = cos_ref[...], sin_ref[...]
    cos = jnp.where(own, c, pltpu.roll(c, half, axis=1))
    sin = jnp.where(own, s, pltpu.roll(s, half, axis=1))
    return cos, jnp.where(low, -sin, sin)


def _ffn1_kernel(x_ref, g1_ref, wg_ref, wu_ref, wd_ref, gmix_ref, *refs):
    n_cast = (len(refs) - 2) // 2
    h_ref, u_ref = refs[n_cast:n_cast + 2]
    n = x_ref.shape[0] // FFN1_PARTS
    for part in range(FFN1_PARTS):
        rows = slice(part * n, (part + 1) * n)
        h = _swiglu_half_step(x_ref[rows, :], g1_ref[...], wg_ref, wu_ref, wd_ref)
        h_ref[rows, :] = h
        u_ref[rows, :] = _rms(h, gmix_ref[...]).astype(BF16)
    _cast_slabs(refs[:n_cast], refs[n_cast + 2:])


def _ffn1(x, g1, wg, wu, wd, gmix, tm, cast_weights):
    t, d = x.shape
    ff = wg.shape[1]
    tile = pl.BlockSpec((tm, d), lambda i: (i, 0))
    steps = t // tm
    c_in, c_out, c_shape = _cast_plan(cast_weights, steps)
    outs = pl.pallas_call(
        _ffn1_kernel,
        grid=(steps,),
        in_specs=[tile, _resident((1, d)), _resident((d, ff)), _resident((d, ff)),
                  _resident((ff, d)), _resident((1, d))] + c_in,
        out_specs=[tile, tile] + c_out,
        out_shape=[jax.ShapeDtypeStruct((t, d), F32), jax.ShapeDtypeStruct((t, d), BF16)] + c_shape,
        compiler_params=_cparams(1),
        name="ffn1",
    )(x, g1, wg, wu, wd, gmix, *cast_weights)
    return outs[0], outs[1], outs[2:]


def _rotary(t, cos, sin_signed):
    return t * cos + pltpu.roll(t, HEAD_DIM // 2, axis=1) * sin_signed


def _conv_lane_chunk(xbuf, w_ref, b_ref, acc_ref, tm, c):
    off = CONV_HALO - (CONV_WIDTH - 1)
    cols = slice(c * LANES, (c + 1) * LANES)
    taps = [w_ref[j:j + 1, cols] for j in range(CONV_WIDTH)]
    bias = b_ref[:, cols]
    for rb in range(tm // CONV_ROWS):
        base = rb * CONV_ROWS
        out = None
        for r in range(SUBLANES):
            n = CONV_ROWS if r == 0 else CONV_ROWS + SUBLANES
            part = None
            for q in range((off + CONV_WIDTH - 1) // SUBLANES + 1):
                j = SUBLANES * q + r - off
                if 0 <= j < CONV_WIDTH:
                    lo = base + SUBLANES * q
                    term = taps[j] * xbuf[lo:lo + n, cols]
                    part = term if part is None else part + term
            shifted = part if r == 0 else part[r:r + CONV_ROWS]
            out = shifted if out is None else out + shifted
        acc_ref[base:base + CONV_ROWS, cols] = out + bias


def _mixer_in_kernel(u_ref, w_ref, cos_ref, sin_ref, wdw_ref, bdw_ref, gln_ref, bln_ref, wco_ref,
                     gyc_ref, q0_ref, q1_ref, q2_ref, k0_ref, k1_ref, k2_ref,
                     v0_ref, v1_ref, v2_ref, gb_ref, xbuf, acc_ref, ga_buf, stage,
                     *, tm, c_conv, d_model):
    def proj(lo, width):
        return _dot(u_ref[...], w_ref[:, lo:lo + width])

    @pl.when(pl.program_id(1) == 0)
    def _():
        xbuf[0:CONV_HALO, :] = jnp.zeros((CONV_HALO, c_conv), F32)

    a = proj(0, c_conv)
    gate = proj(c_conv, c_conv)
    xbuf[CONV_HALO:, :] = a * _sigmoid(gate)

    def emit(out_ref, val, dil):
        if dil == 1:
            out_ref[...] = val.astype(BF16)
            return
        rows = tm // dil
        for c in range(GROUP_W // LANES):
            stage[c] = val[:, c * LANES:(c + 1) * LANES]
        for r in range(dil):
            for c in range(GROUP_W // LANES):
                lo = r * GROUP_W + c * LANES
                out_ref[:, lo:lo + LANES] = stage[c, pl.ds(r, rows, stride=dil), :].astype(BF16)

    base = 2 * c_conv
    gbase = base + 3 * ATTN_W

    tile_idx = pl.program_id(0) * pl.num_programs(1) + pl.program_id(1)
    n_tiles = pl.num_programs(0) * pl.num_programs(1)
    cos, sin = _unpack_rope(cos_ref, sin_ref, 2 * tile_idx >= n_tiles)

    def rotary_piece(out_ref, lo, dil, scale):
        def piece():
            t = proj(lo, GROUP_W)
            heads = [_rotary(t[:, h * HEAD_DIM:(h + 1) * HEAD_DIM], cos, sin)
                     for h in range(HEADS_PER_GROUP)]
            if scale is not None:
                heads = [x * scale for x in heads]
            emit(out_ref, jnp.concatenate(heads, axis=1), dil)
        return piece

    def value_piece(out_ref, lo, dil):
        return lambda: emit(out_ref, proj(lo, GROUP_W), dil)

    def gate_a_piece():
        ga_buf[...] = _sigmoid(proj(gbase, d_model))

    def gate_b_piece():
        gb_ref[...] = _sigmoid(proj(gbase + d_model, d_model)).astype(BF16)

    pieces = [gate_a_piece, gate_b_piece]
    for g, (q_ref, k_ref, v_ref) in enumerate(
            ((q0_ref, k0_ref, v0_ref), (q1_ref, k1_ref, v1_ref), (q2_ref, k2_ref, v2_ref))):
        dil = GROUPS[g][1]
        pieces.append(rotary_piece(q_ref, base + g * GROUP_W, dil, HEAD_DIM ** -0.5))
        pieces.append(rotary_piece(k_ref, base + ATTN_W + g * GROUP_W, dil, None))
        pieces.append(value_piece(v_ref, base + 2 * ATTN_W + g * GROUP_W, dil))

    for c in range(c_conv // LANES):
        _conv_lane_chunk(xbuf, wdw_ref, bdw_ref, acc_ref, tm, c)
    xbuf[0:CONV_HALO, :] = xbuf[tm:tm + CONV_HALO, :]
    for piece in pieces:
        piece()

    acc = acc_ref[...]
    mu = jnp.mean(acc, axis=-1, keepdims=True)
    cen = acc - mu
    var = jnp.mean(cen * cen, axis=-1, keepdims=True)
    y = cen * lax.rsqrt(var + EPS) * gln_ref[...] + bln_ref[...]
    y = (y * _sigmoid(y)).astype(BF16)
    gyc_ref[...] = (ga_buf[...] * _dot(y, wco_ref[...])).astype(BF16)


def _mixer_in(u, w_in, cos, sin, w_dw, b_dw, g_ln, b_ln, w_co, batch, tm):
    t, d = u.shape
    in_w = w_in.shape[1]
    c_conv = w_dw.shape[1]
    per_seq = t // batch // tm
    idx = lambda b, i: (b * per_seq + i, 0)
    row = lambda w: pl.BlockSpec((tm, w), idx)
    half_tiles = t // 2 // tm
    rope = pl.BlockSpec((tm, HEAD_DIM), lambda b, i: (lax.rem(b * per_seq + i, half_tiles), 0))
    strided = lambda dil: pl.BlockSpec((tm // dil, dil * GROUP_W), idx)
    strided_sds = lambda dil: jax.ShapeDtypeStruct((t // dil, dil * GROUP_W), BF16)
    dils = [dil for _, dil in GROUPS]
    out_specs = [row(d)] + [strided(dil) for _ in range(3) for dil in dils] + [row(d)]
    out_shape = ([jax.ShapeDtypeStruct((t, d), BF16)]
                 + [strided_sds(dil) for _ in range(3) for dil in dils]
                 + [jax.ShapeDtypeStruct((t, d), BF16)])
    return pl.pallas_call(
        functools.partial(_mixer_in_kernel, tm=tm, c_conv=c_conv, d_model=d),
        grid=(batch, per_seq),
        in_specs=[row(d), _resident((d, in_w)), rope, rope,
                  _resident((CONV_WIDTH, c_conv)), _resident((1, c_conv)), _resident((1, c_conv)),
                  _resident((1, c_conv)), _resident((c_conv, d))],
        out_specs=out_specs,
        out_shape=out_shape,
        scratch_shapes=[pltpu.VMEM((tm + CONV_HALO, c_conv), F32),
                        pltpu.VMEM((tm, c_conv), F32),
                        pltpu.VMEM((tm, d), F32),
                        pltpu.VMEM((GROUP_W // LANES, tm, LANES), F32)],
        compiler_params=_cparams(2),
        name="mixer_in",
    )(u, w_in, cos, sin, w_dw, b_dw, g_ln, b_ln, w_co)


def _attn_kernel(q_ref, k_ref, v_ref, kh_ref, vh_ref, o_ref, lse_ref, kbuf, vbuf,
                 *, lt, unroll, rps):
    first_tile = pl.program_id(2) == 0
    ones = jnp.ones((lt + Q_BLOCK, HEAD_DIM), BF16)
    for rr in range(rps):
        kbuf[rr, 0:Q_BLOCK, :] = kh_ref[0, :, rr * GROUP_W:(rr + 1) * GROUP_W]
        kbuf[rr, Q_BLOCK:, :] = k_ref[0, :, rr * GROUP_W:(rr + 1) * GROUP_W]
        for h in range(HEADS_PER_GROUP):
            src = slice(rr * GROUP_W + h * HEAD_DIM, rr * GROUP_W + (h + 1) * HEAD_DIM)
            vbuf[rr, 0:Q_BLOCK, 2 * h * HEAD_DIM:(2 * h + 1) * HEAD_DIM] = vh_ref[0, :, src]
            vbuf[rr, Q_BLOCK:, 2 * h * HEAD_DIM:(2 * h + 1) * HEAD_DIM] = v_ref[0, :, src]
            vbuf[rr, :, (2 * h + 1) * HEAD_DIM:(2 * h + 2) * HEAD_DIM] = ones

    qi = lax.broadcasted_iota(jnp.int32, (Q_BLOCK, 2 * Q_BLOCK), 0)
    ki = lax.broadcasted_iota(jnp.int32, (Q_BLOCK, 2 * Q_BLOCK), 1)
    dist = qi + Q_BLOCK - ki
    band = (dist >= 0) & (dist <= Q_BLOCK)
    lane = lax.broadcasted_iota(jnp.int32, (Q_BLOCK, LANES), 1)
    nt = (((1,), (1,)), ((), ()))

    def blocks(units):
        work = [(rr, start, mask, h) for rr, start, mask in units for h in range(HEADS_PER_GROUP)]
        col = lambda rr, h: slice(rr * GROUP_W + h * HEAD_DIM, rr * GROUP_W + (h + 1) * HEAD_DIM)
        s = [lax.dot_general(q_ref[0, pl.ds(start, Q_BLOCK), col(rr, h)],
                             kbuf[rr, pl.ds(start, 2 * Q_BLOCK), col(0, h)], nt,
                             preferred_element_type=F32) for rr, start, _, h in work]
        s = [jnp.where(mask, x, NEG) for x, (_, _, mask, _) in zip(s, work)]
        m = [jnp.max(x, axis=-1, keepdims=True) for x in s]
        e = [jnp.exp(x - mx).astype(BF16) for x, mx in zip(s, m)]
        pv = [_dot(ex, vbuf[rr, pl.ds(start, 2 * Q_BLOCK), 2 * h * HEAD_DIM:(2 * h + 2) * HEAD_DIM])
              for ex, (rr, start, _, h) in zip(e, work)]
        for u, (rr, start, _) in enumerate(units):
            rows = pl.ds(start, Q_BLOCK)
            packed = None
            for h in range(HEADS_PER_GROUP):
                i = u * HEADS_PER_GROUP + h
                den = pv[i][:, HEAD_DIM:]
                o_ref[0, rows, col(rr, h)] = (pv[i][:, :HEAD_DIM] / den).astype(BF16)
                lse = m[i] + jnp.log(den)
                packed = lse if packed is None else jnp.where(lane >= h * STAT_REP, lse, packed)
            lse_ref[0, rows, rr * LANES:(rr + 1) * LANES] = packed

    def run(units):
        for lo in range(0, len(units), unroll):
            blocks(units[lo:lo + unroll])

    per = lt // Q_BLOCK
    upr = min(unroll, per)
    first_mask = band & jnp.logical_or(ki >= Q_BLOCK, jnp.logical_not(first_tile))
    run([(rr, j * Q_BLOCK, first_mask if j == 0 else band)
         for rr in range(rps) for j in range(upr)])

    def body(n, carry):
        start = n * (upr * Q_BLOCK)
        run([(rr, pl.multiple_of(start + j * Q_BLOCK, Q_BLOCK), band)
             for rr in range(rps) for j in range(upr)])
        return carry

    lax.fori_loop(1, per // upr, body, 0)


def _attention_group(q, k, v, batch, dilation, lt, rps):
    rows, _ = q.shape
    l = rows // batch
    lt = min(lt, l)
    per = lt // Q_BLOCK
    unroll = min(ATTN_UNROLL, per * rps)
    assert l % lt == 0 and dilation % rps == 0 and per % min(unroll, per) == 0
    view = lambda x: x.reshape(batch, l, x.shape[-1])
    tile = lambda width: pl.BlockSpec((1, lt, rps * width), lambda bi, r, i: (bi, i, r))
    halo = pl.BlockSpec((1, Q_BLOCK, rps * GROUP_W),
                        lambda bi, r, i: (bi, jnp.maximum(i * per - 1, 0), r))
    o, lse = pl.pallas_call(
        functools.partial(_attn_kernel, lt=lt, unroll=unroll, rps=rps),
        grid=(batch, dilation // rps, l // lt),
        in_specs=[tile(GROUP_W), tile(GROUP_W), tile(GROUP_W), halo, halo],
        out_specs=[tile(GROUP_W), tile(LANES)],
        out_shape=[jax.ShapeDtypeStruct((batch, l, dilation * GROUP_W), BF16),
                   jax.ShapeDtypeStruct((batch, l, dilation * LANES), F32)],
        scratch_shapes=[pltpu.VMEM((rps, lt + Q_BLOCK, GROUP_W), BF16),
                        pltpu.VMEM((rps, lt + Q_BLOCK, 2 * GROUP_W), BF16)],
        compiler_params=_cparams(3),
        name=f"attn_d{dilation}",
    )(view(q), view(k), view(v), view(k), view(v))
    return o.reshape(rows, dilation * GROUP_W), lse.reshape(rows, dilation * LANES)


def _tail_kernel(h_ref, gyc_ref, gb_ref, o0_ref, o1_ref, o2_ref, l0_ref, l1_ref, l2_ref, p_ref,
                 wao_ref, wmix_ref, g2_ref, wg_ref, wu_ref, wd_ref, gple_ref, wpg_ref, wpp_ref,
                 gfin_ref, y_ref, obuf, lbuf, *, tm, final_norm):
    n_parts = TAIL_PARTS
    for part in range(n_parts):
        _tail_rows(part, tm // n_parts, h_ref, gyc_ref, gb_ref, (o0_ref, o1_ref, o2_ref),
                   (l0_ref, l1_ref, l2_ref), p_ref, wao_ref, wmix_ref, g2_ref, wg_ref, wu_ref,
                   wd_ref, gple_ref, wpg_ref, wpp_ref, gfin_ref, y_ref, obuf, lbuf, final_norm)


def _tail_rows(part, tp, h_ref, gyc_ref, gb_ref, o_refs, l_refs, p_ref, wao_ref, wmix_ref, g2_ref,
               wg_ref, wu_ref, wd_ref, gple_ref, wpg_ref, wpp_ref, gfin_ref, y_ref, obuf, lbuf,
               final_norm):
    rows_tok = slice(part * tp, (part + 1) * tp)

    def token_order(dst, src_ref, dil, width):
        chunks = width // LANES
        n = tp // dil
        src_rows = slice(part * n, (part + 1) * n)
        if dil == 1:
            return [src_ref[src_rows, c * LANES:(c + 1) * LANES].astype(F32) for c in range(chunks)]
        for r in range(dil):
            for c in range(chunks):
                lo = r * width + c * LANES
                dst[c, pl.ds(part * tp + r, n, stride=dil), :] = (
                    src_ref[src_rows, lo:lo + LANES].astype(F32))
        return [dst[c, rows_tok, :] for c in range(chunks)]

    dils = [dil for _, dil in GROUPS]
    ls = [token_order(lbuf.at[g], ref, dils[g], LANES)[0] for g, ref in enumerate(l_refs)]
    os_ = [token_order(obuf.at[g], ref, dils[g], GROUP_W) for g, ref in enumerate(o_refs)]

    m = jnp.maximum(jnp.maximum(ls[0], ls[1]), ls[2])
    es = [jnp.exp(x - m) for x in ls]
    inv = 1.0 / (es[0] + es[1] + es[2])
    alphas = [x * inv for x in es]
    src = lax.broadcasted_iota(jnp.int32, (2 * LANES, GROUP_W), 0) % LANES
    dst_head = lax.broadcasted_iota(jnp.int32, (2 * LANES, GROUP_W), 1) // HEAD_DIM
    spread = jnp.where(src == dst_head * STAT_REP, 1.0, 0.0).astype(BF16)
    wide = []
    for a in alphas:
        hi = a.astype(BF16)
        lo = (a - hi.astype(F32)).astype(BF16)
        wide.append(_dot(jnp.concatenate([hi, lo], axis=1), spread))
    heads = []
    for h in range(HEADS_PER_GROUP):
        acc = None
        for g in range(N_GROUPS):
            term = wide[g][:, h * HEAD_DIM:(h + 1) * HEAD_DIM] * os_[g][h]
            acc = term if acc is None else acc + term
        heads.append(acc.astype(BF16))
    o = jnp.concatenate(heads, axis=1)
    y_attn = _dot(o, wao_ref[...])
    mixed = gyc_ref[rows_tok, :].astype(F32) + gb_ref[rows_tok, :].astype(F32) * y_attn
    h = h_ref[rows_tok, :] + _dot(mixed.astype(BF16), wmix_ref[...])
    h = _swiglu_half_step(h, g2_ref[...], wg_ref, wu_ref, wd_ref)
    gate = _sigmoid(_dot(_rms(h, gple_ref[...]).astype(BF16), wpg_ref[...]))
    h = h + gate * _dot(p_ref[rows_tok, :].astype(BF16), wpp_ref[...])
    y_ref[rows_tok, :] = _rms(h, gfin_ref[...]) if final_norm else h


def _tail(h, gyc, gb, os_, ls_, p, wao, wmix, g2, wg, wu, wd, gple, wpg, wpp, gfin, tm,
          final_norm):
    t, d = h.shape
    row = lambda w: pl.BlockSpec((tm, w), lambda i: (i, 0))
    strided = lambda dil, w: pl.BlockSpec((tm // dil, dil * w), lambda i: (i, 0))
    res = lambda x: _resident(x.shape)
    weights = (wao, wmix, g2, wg, wu, wd, gple, wpg, wpp, gfin)
    dils = [dil for _, dil in GROUPS]
    return pl.pallas_call(
        functools.partial(_tail_kernel, tm=tm, final_norm=final_norm),
        grid=(t // tm,),
        in_specs=[row(d), row(d), row(d)] + [strided(dil, GROUP_W) for dil in dils]
                 + [strided(dil, LANES) for dil in dils]
                 + [row(p.shape[1])] + [res(x) for x in weights],
        out_specs=row(d),
        out_shape=jax.ShapeDtypeStruct((t, d), F32),
        scratch_shapes=[pltpu.VMEM((N_GROUPS, GROUP_W // LANES, tm, LANES), F32),
                        pltpu.VMEM((N_GROUPS, 1, tm, LANES), F32)],
        compiler_params=_cparams(1),
        name="tail",
    )(h, gyc, gb, *os_, *ls_, p, *weights)


def kernel(x, p, positions, g_ffn1, w_ffn1_gate, w_ffn1_up, w_ffn1_down, g_mix, w_in,
           w_dw, b_dw, g_conv_ln, b_conv_ln, w_conv_out, w_attn_out, w_mix_out,
           g_ffn2, w_ffn2_gate, w_ffn2_up, w_ffn2_down, g_ple, w_ple_gate, w_ple_proj,
           g_final):
    b, s, d = x.shape
    depth = p.shape[0]
    t = b * s
    c_conv = w_dw.shape[-1]
    assert w_in.shape[-1] == 2 * c_conv + 3 * ATTN_W + 2 * d
    tm, tm_mix = 512, 256
    max_dil = max(dil for _, dil in GROUPS)
    assert s % tm == 0 and s % tm_mix == 0
    assert tm_mix % CONV_ROWS == 0 and (tm_mix // max_dil) % 16 == 0
    for window, dil in GROUPS:
        assert window // dil == Q_BLOCK and (s // dil) % Q_BLOCK == 0

    assert depth == 1
    i = 0
    row = lambda v: v.reshape(1, -1)
    cos, sin, (wg1, wu1, wd1, wd2, wpp) = _rope_tables(
        positions, tm,
        [w_ffn1_gate[i], w_ffn1_up[i], w_ffn1_down[i], w_ffn2_down[i], w_ple_proj[i]])
    h, u, (win, wco, wao, wmix, wg2, wu2, wpg) = _ffn1(
        x.reshape(t, d), row(g_ffn1[i]), wg1, wu1, wd1, row(g_mix[i]), tm,
        [w_in[i], w_conv_out[i], w_attn_out[i], w_mix_out[i], w_ffn2_gate[i], w_ffn2_up[i],
         w_ple_gate[i]])
    gyc, q0, q1, q2, k0, k1, k2, v0, v1, v2, gb = _mixer_in(
        u, win, cos, sin, w_dw[i], row(b_dw[i]), row(g_conv_ln[i]), row(b_conv_ln[i]), wco,
        b, tm_mix)
    os_, ls_ = [], []
    for (_, dil), q, k, v in zip(GROUPS, (q0, q1, q2), (k0, k1, k2), (v0, v1, v2)):
        o, lse = _attention_group(q, k, v, b, dil, ATTN_ROWS, max(1, ATTN_ROWS * dil // s))
        os_.append(o)
        ls_.append(lse)
    h = _tail(h, gyc, gb, os_, ls_, p[i].reshape(t, -1), wao, wmix, row(g_ffn2[i]), wg2, wu2,
              wd2, row(g_ple[i]), wpg, wpp, row(g_final), tm, True)
    return h.reshape(b, s, d)
```

```python
import functools

import jax
import jax.numpy as jnp
from jax import lax
from jax.experimental import pallas as pl
from jax.experimental.pallas import tpu as pltpu

HEAD_DIM = 128
HEADS_PER_GROUP = 4
GROUPS = ((128, 1), (512, 4), (2048, 16))
N_GROUPS = len(GROUPS)
GROUP_W = HEADS_PER_GROUP * HEAD_DIM
ATTN_W = N_GROUPS * GROUP_W
CONV_WIDTH = 31
Q_BLOCK = 128
ROPE_THETA = 10000.0
EPS = 1e-6
NEG = -1e30

LANES = 128
SUBLANES = 8
BF16_ROWS = 16
MXU_COLS = 256
CONV_HALO = 32
CONV_ROWS = 128
ATTN_UNROLL = 4
ATTN_ROWS = 1024
FFN1_PARTS = 2
TAIL_PARTS = 1
STAT_REP = LANES // HEADS_PER_GROUP
VMEM_LIMIT = 56 * 1024 * 1024

BF16 = jnp.bfloat16
F32 = jnp.float32


def _cparams(n_axes):
    return pltpu.CompilerParams(
        dimension_semantics=("arbitrary",) * n_axes, vmem_limit_bytes=VMEM_LIMIT)


def _resident(shape):
    nd = len(shape)
    return pl.BlockSpec(shape, lambda *_: (0,) * nd, pipeline_mode=pl.Buffered(1))


def _rms(x, g):
    ms = jnp.mean(x * x, axis=-1, keepdims=True)
    return x * lax.rsqrt(ms + EPS) * g


def _sigmoid(x):
    return 1.0 / (1.0 + jnp.exp(-x))


def _dot(a, b):
    return jnp.dot(a, b, preferred_element_type=F32)


def _swiglu_half_step(x, g, wg_ref, wu_ref, wd_ref):
    xn = _rms(x, g).astype(BF16)
    ff = wg_ref.shape[1]
    acts = []
    for lo in range(0, ff, MXU_COLS):
        gate = _dot(xn, wg_ref[:, lo:lo + MXU_COLS])
        up = _dot(xn, wu_ref[:, lo:lo + MXU_COLS])
        acts.append((gate * _sigmoid(gate) * up).astype(BF16))
    act = jnp.concatenate(acts, axis=1)
    return x + 0.5 * _dot(act, wd_ref[...])


def _cast_plan(weights, steps):
    in_specs, out_specs, out_shape = [], [], []
    for w in weights:
        rows, cols = w.shape
        slab = rows // steps
        assert rows % steps == 0 and slab % BF16_ROWS == 0, (w.shape, steps)
        spec = pl.BlockSpec((slab, cols), lambda i: (i, 0))
        in_specs.append(spec)
        out_specs.append(spec)
        out_shape.append(jax.ShapeDtypeStruct((rows, cols), BF16))
    return in_specs, out_specs, out_shape


def _cast_slabs(src_refs, dst_refs):
    for src, dst in zip(src_refs, dst_refs):
        dst[...] = src[...].astype(BF16)


def _rope_kernel(pos_lo_ref, pos_hi_ref, freq_ref, *refs):
    n_cast = (len(refs) - 2) // 2
    cos_ref, sin_ref = refs[n_cast:n_cast + 2]
    lane = lax.broadcasted_iota(jnp.int32, cos_ref.shape, 1)
    pos = jnp.where(lane < HEAD_DIM // 2, pos_lo_ref[...], pos_hi_ref[...])
    ang = pos.astype(F32) * freq_ref[...]
    cos_ref[...] = jnp.cos(ang)
    sin_ref[...] = jnp.sin(ang)
    _cast_slabs(refs[:n_cast], refs[n_cast + 2:])


def _rope_tables(positions, tm, cast_weights):
    t = positions.size
    half_t = t // 2
    inv_freq = ROPE_THETA ** (-jnp.arange(0, HEAD_DIM, 2, dtype=F32) / HEAD_DIM)
    freq = jnp.concatenate([inv_freq, inv_freq])[None, :]
    pos = positions.reshape(2, half_t, 1)
    tile = pl.BlockSpec((tm, HEAD_DIM), lambda i: (i, 0))
    col = pl.BlockSpec((tm, 1), lambda i: (i, 0))
    steps = half_t // tm
    c_in, c_out, c_shape = _cast_plan(cast_weights, steps)
    outs = pl.pallas_call(
        _rope_kernel,
        grid=(steps,),
        in_specs=[col, col, pl.BlockSpec((1, HEAD_DIM), lambda i: (0, 0))] + c_in,
        out_specs=[tile, tile] + c_out,
        out_shape=[jax.ShapeDtypeStruct((half_t, HEAD_DIM), F32)] * 2 + c_shape,
        compiler_params=_cparams(1),
        name="rope_tables",
    )(pos[0], pos[1], freq, *cast_weights)
    return outs[0], outs[1], outs[2:]


def _unpack_rope(cos_ref, sin_ref, upper_half):
    half = HEAD_DIM // 2
    lane = lax.broadcasted_iota(jnp.int32, cos_ref.shape, 1)
    low = lane < half
    own = jnp.logical_xor(low, upper_half)
    c, s = cos_ref[...], sin_ref[...]
    cos = jnp.where(own, c, pltpu.roll(c, half, axis=1))
    sin = jnp.where(own, s, pltpu.roll(s, half, axis=1))
    return cos, jnp.where(low, -sin, sin)


def _ffn1_kernel(x_ref, g1_ref, wg_ref, wu_ref, wd_ref, gmix_ref, *refs):
    n_cast = (len(refs) - 2) // 2
    h_ref, u_ref = refs[n_cast:n_cast + 2]
    n = x_ref.shape[0] // FFN1_PARTS
    for part in range(FFN1_PARTS):
        rows = slice(part * n, (part + 1) * n)
        h = _swiglu_half_step(x_ref[rows, :], g1_ref[...], wg_ref, wu_ref, wd_ref)
        h_ref[rows, :] = h
        u_ref[rows, :] = _rms(h, gmix_ref[...]).astype(BF16)
    _cast_slabs(refs[:n_cast], refs[n_cast + 2:])


def _ffn1(x, g1, wg, wu, wd, gmix, tm, cast_weights):
    t, d = x.shape
    ff = wg.shape[1]
    tile = pl.BlockSpec((tm, d), lambda i: (i, 0))
    steps = t // tm
    c_in, c_out, c_shape = _cast_plan(cast_weights, steps)
    outs = pl.pallas_call(
        _ffn1_kernel,
        grid=(steps,),
        in_specs=[tile, _resident((1, d)), _resident((d, ff)), _resident((d, ff)),
                  _resident((ff, d)), _resident((1, d))] + c_in,
        out_specs=[tile, tile] + c_out,
        out_shape=[jax.ShapeDtypeStruct((t, d), F32), jax.ShapeDtypeStruct((t, d), BF16)] + c_shape,
        compiler_params=_cparams(1),
        name="ffn1",
    )(x, g1, wg, wu, wd, gmix, *cast_weights)
    return outs[0], outs[1], outs[2:]


def _rotary(t, cos, sin_signed):
    return t * cos + pltpu.roll(t, HEAD_DIM // 2, axis=1) * sin_signed


CONV_OFF = CONV_HALO - (CONV_WIDTH - 1)
CONV_SLABS = 4
CONV_MXU_PHASES = (2, 3, 4, 5, 6, 7)


def _build_conv_diag(w_ref, dmat):
    eye = (lax.broadcasted_iota(jnp.int32, (LANES, LANES), 0)
           == lax.broadcasted_iota(jnp.int32, (LANES, LANES), 1))
    for c in range(dmat.shape[0]):
        for q in range(CONV_SLABS):
            for g, r in enumerate(CONV_MXU_PHASES):
                j = SUBLANES * q + r - CONV_OFF
                assert 0 <= j < CONV_WIDTH
                tap = w_ref[j:j + 1, c * LANES:(c + 1) * LANES]
                dmat[c, q * LANES:(q + 1) * LANES, g * LANES:(g + 1) * LANES] = (
                    jnp.where(eye, tap, 0.0).astype(BF16))


def _conv_lane_chunk(xbuf, w_ref, b_ref, dmat, acc_ref, tm, c):
    cols = slice(c * LANES, (c + 1) * LANES)
    taps = [w_ref[j:j + 1, cols] for j in range(CONV_WIDTH)]
    bias = b_ref[:, cols]
    m_rows = tm + BF16_ROWS
    slabs = [xbuf[SUBLANES * q:SUBLANES * q + m_rows, cols].astype(BF16) for q in range(CONV_SLABS)]
    phase_sums = _dot(jnp.concatenate(slabs, axis=1), dmat[c])
    for rb in range(tm // CONV_ROWS):
        base = rb * CONV_ROWS
        out = None
        for r in range(SUBLANES):
            if r in CONV_MXU_PHASES:
                g = CONV_MXU_PHASES.index(r)
                shifted = phase_sums[base + r:base + r + CONV_ROWS, g * LANES:(g + 1) * LANES]
            else:
                n = CONV_ROWS if r == 0 else CONV_ROWS + SUBLANES
                part = None
                for q in range((CONV_OFF + CONV_WIDTH - 1) // SUBLANES + 1):
                    j = SUBLANES * q + r - CONV_OFF
                    if 0 <= j < CONV_WIDTH:
                        lo = base + SUBLANES * q
                        term = taps[j] * xbuf[lo:lo + n, cols]
                        part = term if part is None else part + term
                shifted = part if r == 0 else part[r:r + CONV_ROWS]
            out = shifted if out is None else out + shifted
        acc_ref[base:base + CONV_ROWS, cols] = out + bias


def _mixer_in_kernel(u_ref, w_ref, cos_ref, sin_ref, wdw_ref, bdw_ref, gln_ref, bln_ref, wco_ref,
                     gyc_ref, q0_ref, q1_ref, q2_ref, k0_ref, k1_ref, k2_ref,
                     v0_ref, v1_ref, v2_ref, gb_ref, xbuf, acc_ref, ga_buf, stage, dmat,
                     *, tm, c_conv, d_model):
    def proj(lo, width):
        return _dot(u_ref[...], w_ref[:, lo:lo + width])

    @pl.when((pl.program_id(0) == 0) & (pl.program_id(1) == 0))
    def _():
        _build_conv_diag(wdw_ref, dmat)
        xbuf[CONV_HALO + tm:, :] = jnp.zeros((BF16_ROWS, c_conv), F32)

    @pl.when(pl.program_id(1) == 0)
    def _():
        xbuf[0:CONV_HALO, :] = jnp.zeros((CONV_HALO, c_conv), F32)

    a = proj(0, c_conv)
    gate = proj(c_conv, c_conv)
    xbuf[CONV_HALO:CONV_HALO + tm, :] = a * _sigmoid(gate)

    def emit(out_ref, val, dil):
        if dil == 1:
            out_ref[...] = val.astype(BF16)
            return
        rows = tm // dil
        for c in range(GROUP_W // LANES):
            stage[c] = val[:, c * LANES:(c + 1) * LANES]
        for r in range(dil):
            for c in range(GROUP_W // LANES):
                lo = r * GROUP_W + c * LANES
                out_ref[:, lo:lo + LANES] = stage[c, pl.ds(r, rows, stride=dil), :].astype(BF16)

    base = 2 * c_conv
    gbase = base + 3 * ATTN_W

    tile_idx = pl.program_id(0) * pl.num_programs(1) + pl.program_id(1)
    n_tiles = pl.num_programs(0) * pl.num_programs(1)
    cos, sin = _unpack_rope(cos_ref, sin_ref, 2 * tile_idx >= n_tiles)

    def rotary_piece(out_ref, lo, dil, scale):
        def piece():
            t = proj(lo, GROUP_W)
            heads = [_rotary(t[:, h * HEAD_DIM:(h + 1) * HEAD_DIM], cos, sin)
                     for h in range(HEADS_PER_GROUP)]
            if scale is not None:
                heads = [x * scale for x in heads]
            emit(out_ref, jnp.concatenate(heads, axis=1), dil)
        return piece

    def value_piece(out_ref, lo, dil):
        return lambda: emit(out_ref, proj(lo, GROUP_W), dil)

    def gate_a_piece():
        ga_buf[...] = _sigmoid(proj(gbase, d_model))

    def gate_b_piece():
        gb_ref[...] = _sigmoid(proj(gbase + d_model, d_model)).astype(BF16)

    pieces = [gate_a_piece, gate_b_piece]
    for g, (q_ref, k_ref, v_ref) in enumerate(
            ((q0_ref, k0_ref, v0_ref), (q1_ref, k1_ref, v1_ref), (q2_ref, k2_ref, v2_ref))):
        dil = GROUPS[g][1]
        pieces.append(rotary_piece(q_ref, base + g * GROUP_W, dil, HEAD_DIM ** -0.5))
        pieces.append(rotary_piece(k_ref, base + ATTN_W + g * GROUP_W, dil, None))
        pieces.append(value_piece(v_ref, base + 2 * ATTN_W + g * GROUP_W, dil))

    for c in range(c_conv // LANES):
        _conv_lane_chunk(xbuf, wdw_ref, bdw_ref, dmat, acc_ref, tm, c)
    xbuf[0:CONV_HALO, :] = xbuf[tm:tm + CONV_HALO, :]
    for piece in pieces:
        piece()

    acc = acc_ref[...]
    mu = jnp.mean(acc, axis=-1, keepdims=True)
    cen = acc - mu
    var = jnp.mean(cen * cen, axis=-1, keepdims=True)
    y = cen * lax.rsqrt(var + EPS) * gln_ref[...] + bln_ref[...]
    y = (y * _sigmoid(y)).astype(BF16)
    gyc_ref[...] = (ga_buf[...] * _dot(y, wco_ref[...])).astype(BF16)


def _mixer_in(u, w_in, cos, sin, w_dw, b_dw, g_ln, b_ln, w_co, batch, tm):
    t, d = u.shape
    in_w = w_in.shape[1]
    c_conv = w_dw.shape[1]
    per_seq = t // batch // tm
    idx = lambda b, i: (b * per_seq + i, 0)
    row = lambda w: pl.BlockSpec((tm, w), idx)
    half_tiles = t // 2 // tm
    rope = pl.BlockSpec((tm, HEAD_DIM), lambda b, i: (lax.rem(b * per_seq + i, half_tiles), 0))
    strided = lambda dil: pl.BlockSpec((tm // dil, dil * GROUP_W), idx)
    strided_sds = lambda dil: jax.ShapeDtypeStruct((t // dil, dil * GROUP_W), BF16)
    dils = [dil for _, dil in GROUPS]
    out_specs = [row(d)] + [strided(dil) for _ in range(3) for dil in dils] + [row(d)]
    out_shape = ([jax.ShapeDtypeStruct((t, d), BF16)]
                 + [strided_sds(dil) for _ in range(3) for dil in dils]
                 + [jax.ShapeDtypeStruct((t, d), BF16)])
    return pl.pallas_call(
        functools.partial(_mixer_in_kernel, tm=tm, c_conv=c_conv, d_model=d),
        grid=(batch, per_seq),
        in_specs=[row(d), _resident((d, in_w)), rope, rope,
                  _resident((CONV_WIDTH, c_conv)), _resident((1, c_conv)), _resident((1, c_conv)),
                  _resident((1, c_conv)), _resident((c_conv, d))],
        out_specs=out_specs,
        out_shape=out_shape,
        scratch_shapes=[pltpu.VMEM((CONV_HALO + tm + BF16_ROWS, c_conv), F32),
                        pltpu.VMEM((tm, c_conv), F32),
                        pltpu.VMEM((tm, d), F32),
                        pltpu.VMEM((GROUP_W // LANES, tm, LANES), F32),
                        pltpu.VMEM((c_conv // LANES, CONV_SLABS * LANES,
                                    len(CONV_MXU_PHASES) * LANES), BF16)],
        compiler_params=_cparams(2),
        name="mixer_in",
    )(u, w_in, cos, sin, w_dw, b_dw, g_ln, b_ln, w_co)


def _attn_kernel(q_ref, k_ref, v_ref, kh_ref, vh_ref, o_ref, lse_ref, kbuf, vbuf,
                 *, lt, unroll, rps):
    first_tile = pl.program_id(2) == 0
    ones = jnp.ones((lt + Q_BLOCK, HEAD_DIM), BF16)
    for rr in range(rps):
        kbuf[rr, 0:Q_BLOCK, :] = kh_ref[0, :, rr * GROUP_W:(rr + 1) * GROUP_W]
        kbuf[rr, Q_BLOCK:, :] = k_ref[0, :, rr * GROUP_W:(rr + 1) * GROUP_W]
        for h in range(HEADS_PER_GROUP):
            src = slice(rr * GROUP_W + h * HEAD_DIM, rr * GROUP_W + (h + 1) * HEAD_DIM)
            vbuf[rr, 0:Q_BLOCK, 2 * h * HEAD_DIM:(2 * h + 1) * HEAD_DIM] = vh_ref[0, :, src]
            vbuf[rr, Q_BLOCK:, 2 * h * HEAD_DIM:(2 * h + 1) * HEAD_DIM] = v_ref[0, :, src]
            vbuf[rr, :, (2 * h + 1) * HEAD_DIM:(2 * h + 2) * HEAD_DIM] = ones

    qi = lax.broadcasted_iota(jnp.int32, (Q_BLOCK, 2 * Q_BLOCK), 0)
    ki = lax.broadcasted_iota(jnp.int32, (Q_BLOCK, 2 * Q_BLOCK), 1)
    dist = qi + Q_BLOCK - ki
    band = (dist >= 0) & (dist <= Q_BLOCK)
    lane = lax.broadcasted_iota(jnp.int32, (Q_BLOCK, LANES), 1)
    nt = (((1,), (1,)), ((), ()))

    def blocks(units):
        work = [(rr, start, mask, h) for rr, start, mask in units for h in range(HEADS_PER_GROUP)]
        col = lambda rr, h: slice(rr * GROUP_W + h * HEAD_DIM, rr * GROUP_W + (h + 1) * HEAD_DIM)
        s = [lax.dot_general(q_ref[0, pl.ds(start, Q_BLOCK), col(rr, h)],
                             kbuf[rr, pl.ds(start, 2 * Q_BLOCK), col(0, h)], nt,
                             preferred_element_type=F32) for rr, start, _, h in work]
        s = [jnp.where(mask, x, NEG) for x, (_, _, mask, _) in zip(s, work)]
        m = [jnp.max(x, axis=-1, keepdims=True) for x in s]
        e = [jnp.exp(x - mx).astype(BF16) for x, mx in zip(s, m)]
        pv = [_dot(ex, vbuf[rr, pl.ds(start, 2 * Q_BLOCK), 2 * h * HEAD_DIM:(2 * h + 2) * HEAD_DIM])
              for ex, (rr, start, _, h) in zip(e, work)]
        for u, (rr, start, _) in enumerate(units):
            rows = pl.ds(start, Q_BLOCK)
            packed = None
            for h in range(HEADS_PER_GROUP):
                i = u * HEADS_PER_GROUP + h
                den = pv[i][:, HEAD_DIM:]
                o_ref[0, rows, col(rr, h)] = (pv[i][:, :HEAD_DIM] / den).astype(BF16)
                lse = m[i] + jnp.log(den)
                packed = lse if packed is None else jnp.where(lane >= h * STAT_REP, lse, packed)
            lse_ref[0, rows, rr * LANES:(rr + 1) * LANES] = packed

    def run(units):
        for lo in range(0, len(units), unroll):
            blocks(units[lo:lo + unroll])

    per = lt // Q_BLOCK
    upr = min(unroll, per)
    first_mask = band & jnp.logical_or(ki >= Q_BLOCK, jnp.logical_not(first_tile))
    run([(rr, j * Q_BLOCK, first_mask if j == 0 else band)
         for rr in range(rps) for j in range(upr)])

    def body(n, carry):
        start = n * (upr * Q_BLOCK)
        run([(rr, pl.multiple_of(start + j * Q_BLOCK, Q_BLOCK), band)
             for rr in range(rps) for j in range(upr)])
        return carry

    lax.fori_loop(1, per // upr, body, 0)


def _attention_group(q, k, v, batch, dilation, lt, rps):
    rows, _ = q.shape
    l = rows // batch
    lt = min(lt, l)
    per = lt // Q_BLOCK
    unroll = min(ATTN_UNROLL, per * rps)
    assert l % lt == 0 and dilation % rps == 0 and per % min(unroll, per) == 0
    view = lambda x: x.reshape(batch, l, x.shape[-1])
    tile = lambda width: pl.BlockSpec((1, lt, rps * width), lambda bi, r, i: (bi, i, r))
    halo = pl.BlockSpec((1, Q_BLOCK, rps * GROUP_W),
                        lambda bi, r, i: (bi, jnp.maximum(i * per - 1, 0), r))
    o, lse = pl.pallas_call(
        functools.partial(_attn_kernel, lt=lt, unroll=unroll, rps=rps),
        grid=(batch, dilation // rps, l // lt),
        in_specs=[tile(GROUP_W), tile(GROUP_W), tile(GROUP_W), halo, halo],
        out_specs=[tile(GROUP_W), tile(LANES)],
        out_shape=[jax.ShapeDtypeStruct((batch, l, dilation * GROUP_W), BF16),
                   jax.ShapeDtypeStruct((batch, l, dilation * LANES), F32)],
        scratch_shapes=[pltpu.VMEM((rps, lt + Q_BLOCK, GROUP_W), BF16),
                        pltpu.VMEM((rps, lt + Q_BLOCK, 2 * GROUP_W), BF16)],
        compiler_params=_cparams(3),
        name=f"attn_d{dilation}",
    )(view(q), view(k), view(v), view(k), view(v))
    return o.reshape(rows, dilation * GROUP_W), lse.reshape(rows, dilation * LANES)


def _tail_kernel(h_ref, gyc_ref, gb_ref, o0_ref, o1_ref, o2_ref, l0_ref, l1_ref, l2_ref, p_ref,
                 wao_ref, wmix_ref, g2_ref, wg_ref, wu_ref, wd_ref, gple_ref, wpg_ref, wpp_ref,
                 gfin_ref, y_ref, obuf, lbuf, *, tm, final_norm):
    n_parts = TAIL_PARTS
    for part in range(n_parts):
        _tail_rows(part, tm // n_parts, h_ref, gyc_ref, gb_ref, (o0_ref, o1_ref, o2_ref),
                   (l0_ref, l1_ref, l2_ref), p_ref, wao_ref, wmix_ref, g2_ref, wg_ref, wu_ref,
                   wd_ref, gple_ref, wpg_ref, wpp_ref, gfin_ref, y_ref, obuf, lbuf, final_norm)


def _tail_rows(part, tp, h_ref, gyc_ref, gb_ref, o_refs, l_refs, p_ref, wao_ref, wmix_ref, g2_ref,
               wg_ref, wu_ref, wd_ref, gple_ref, wpg_ref, wpp_ref, gfin_ref, y_ref, obuf, lbuf,
               final_norm):
    rows_tok = slice(part * tp, (part + 1) * tp)

    def token_order(dst, src_ref, dil, width):
        chunks = width // LANES
        n = tp // dil
        src_rows = slice(part * n, (part + 1) * n)
        if dil == 1:
            return [src_ref[src_rows, c * LANES:(c + 1) * LANES].astype(F32) for c in range(chunks)]
        for r in range(dil):
            for c in range(chunks):
                lo = r * width + c * LANES
                dst[c, pl.ds(part * tp + r, n, stride=dil), :] = (
                    src_ref[src_rows, lo:lo + LANES].astype(F32))
        return [dst[c, rows_tok, :] for c in range(chunks)]

    dils = [dil for _, dil in GROUPS]
    ls = [token_order(lbuf.at[g], ref, dils[g], LANES)[0] for g, ref in enumerate(l_refs)]
    os_ = [token_order(obuf.at[g], ref, dils[g], GROUP_W) for g, ref in enumerate(o_refs)]

    m = jnp.maximum(jnp.maximum(ls[0], ls[1]), ls[2])
    es = [jnp.exp(x - m) for x in ls]
    inv = 1.0 / (es[0] + es[1] + es[2])
    alphas = [x * inv for x in es]
    src = lax.broadcasted_iota(jnp.int32, (2 * LANES, GROUP_W), 0) % LANES
    dst_head = lax.broadcasted_iota(jnp.int32, (2 * LANES, GROUP_W), 1) // HEAD_DIM
    spread = jnp.where(src == dst_head * STAT_REP, 1.0, 0.0).astype(BF16)
    wide = []
    for a in alphas:
        hi = a.astype(BF16)
        lo = (a - hi.astype(F32)).astype(BF16)
        wide.append(_dot(jnp.concatenate([hi, lo], axis=1), spread))
    heads = []
    for h in range(HEADS_PER_GROUP):
        acc = None
        for g in range(N_GROUPS):
            term = wide[g][:, h * HEAD_DIM:(h + 1) * HEAD_DIM] * os_[g][h]
            acc = term if acc is None else acc + term
        heads.append(acc.astype(BF16))
    o = jnp.concatenate(heads, axis=1)
    y_attn = _dot(o, wao_ref[...])
    mixed = gyc_ref[rows_tok, :].astype(F32) + gb_ref[rows_tok, :].astype(F32) * y_attn
    h = h_ref[rows_tok, :] + _dot(mixed.astype(BF16), wmix_ref[...])
    h = _swiglu_half_step(h, g2_ref[...], wg_ref, wu_ref, wd_ref)
    gate = _sigmoid(_dot(_rms(h, gple_ref[...]).astype(BF16), wpg_ref[...]))
    h = h + gate * _dot(p_ref[rows_tok, :].astype(BF16), wpp_ref[...])
    y_ref[rows_tok, :] = _rms(h, gfin_ref[...]) if final_norm else h


def _tail(h, gyc, gb, os_, ls_, p, wao, wmix, g2, wg, wu, wd, gple, wpg, wpp, gfin, tm,
          final_norm):
    t, d = h.shape
    row = lambda w: pl.BlockSpec((tm, w), lambda i: (i, 0))
    strided = lambda dil, w: pl.BlockSpec((tm // dil, dil * w), lambda i: (i, 0))
    res = lambda x: _resident(x.shape)
    weights = (wao, wmix, g2, wg, wu, wd, gple, wpg, wpp, gfin)
    dils = [dil for _, dil in GROUPS]
    return pl.pallas_call(
        functools.partial(_tail_kernel, tm=tm, final_norm=final_norm),
        grid=(t // tm,),
        in_specs=[row(d), row(d), row(d)] + [strided(dil, GROUP_W) for dil in dils]
                 + [strided(dil, LANES) for dil in dils]
                 + [row(p.shape[1])] + [res(x) for x in weights],
        out_specs=row(d),
        out_shape=jax.ShapeDtypeStruct((t, d), F32),
        scratch_shapes=[pltpu.VMEM((N_GROUPS, GROUP_W // LANES, tm, LANES), F32),
                        pltpu.VMEM((N_GROUPS, 1, tm, LANES), F32)],
        compiler_params=_cparams(1),
        name="tail",
    )(h, gyc, gb, *os_, *ls_, p, *weights)


def kernel(x, p, positions, g_ffn1, w_ffn1_gate, w_ffn1_up, w_ffn1_down, g_mix, w_in,
           w_dw, b_dw, g_conv_ln, b_conv_ln, w_conv_out, w_attn_out, w_mix_out,
           g_ffn2, w_ffn2_gate, w_ffn2_up, w_ffn2_down, g_ple, w_ple_gate, w_ple_proj,
           g_final):
    b, s, d = x.shape
    depth = p.shape[0]
    t = b * s
    c_conv = w_dw.shape[-1]
    assert w_in.shape[-1] == 2 * c_conv + 3 * ATTN_W + 2 * d
    tm, tm_mix = 512, 256
    max_dil = max(dil for _, dil in GROUPS)
    assert s % tm == 0 and s % tm_mix == 0
    assert tm_mix % CONV_ROWS == 0 and (tm_mix // max_dil) % 16 == 0
    for window, dil in GROUPS:
        assert window // dil == Q_BLOCK and (s // dil) % Q_BLOCK == 0

    assert depth == 1
    i = 0
    row = lambda v: v.reshape(1, -1)
    cos, sin, (wg1, wu1, wd1, wd2, wpp) = _rope_tables(
        positions, tm,
        [w_ffn1_gate[i], w_ffn1_up[i], w_ffn1_down[i], w_ffn2_down[i], w_ple_proj[i]])
    h, u, (win, wco, wao, wmix, wg2, wu2, wpg) = _ffn1(
        x.reshape(t, d), row(g_ffn1[i]), wg1, wu1, wd1, row(g_mix[i]), tm,
        [w_in[i], w_conv_out[i], w_attn_out[i], w_mix_out[i], w_ffn2_gate[i], w_ffn2_up[i],
         w_ple_gate[i]])
    gyc, q0, q1, q2, k0, k1, k2, v0, v1, v2, gb = _mixer_in(
        u, win, cos, sin, w_dw[i], row(b_dw[i]), row(g_conv_ln[i]), row(b_conv_ln[i]), wco,
        b, tm_mix)
    os_, ls_ = [], []
    for (_, dil), q, k, v in zip(GROUPS, (q0, q1, q2), (k0, k1, k2), (v0, v1, v2)):
        o, lse = _attention_group(q, k, v, b, dil, ATTN_ROWS, max(1, ATTN_ROWS * dil // s))
        os_.append(o)
        ls_.append(lse)
    h = _tail(h, gyc, gb, os_, ls_, p[i].reshape(t, -1), wao, wmix, row(g_ffn2[i]), wg2, wu2,
              wd2, row(g_ple[i]), wpg, wpp, row(g_final), tm, True)
    return h.reshape(b, s, d)
```

```python
import functools

import jax
import jax.numpy as jnp
from jax import lax
from jax.experimental import pallas as pl
from jax.experimental.pallas import tpu as pltpu

HEAD_DIM = 128
HEADS_PER_GROUP = 4
GROUPS = ((128, 1), (512, 4), (2048, 16))
N_GROUPS = len(GROUPS)
GROUP_W = HEADS_PER_GROUP * HEAD_DIM
ATTN_W = N_GROUPS * GROUP_W
CONV_WIDTH = 31
Q_BLOCK = 128
ROPE_THETA = 10000.0
EPS = 1e-6
NEG = -1e30

LANES = 128
SUBLANES = 8
BF16_ROWS = 16
MXU_COLS = 256
CONV_HALO = 32
CONV_ROWS = 128
ATTN_UNROLL = 4
ATTN_ROWS = 2048
FFN1_PARTS = 2
TAIL_PARTS = 1
STAT_REP = LANES // HEADS_PER_GROUP
VMEM_LIMIT = 56 * 1024 * 1024

BF16 = jnp.bfloat16
F32 = jnp.float32


def _cparams(n_axes):
    return pltpu.CompilerParams(
        dimension_semantics=("arbitrary",) * n_axes, vmem_limit_bytes=VMEM_LIMIT)


def _resident(shape):
    nd = len(shape)
    return pl.BlockSpec(shape, lambda *_: (0,) * nd, pipeline_mode=pl.Buffered(1))


def _rms(x, g):
    ms = jnp.mean(x * x, axis=-1, keepdims=True)
    return x * lax.rsqrt(ms + EPS) * g


def _sigmoid(x):
    return 1.0 / (1.0 + jnp.exp(-x))


def _dot(a, b):
    return jnp.dot(a, b, preferred_element_type=F32)


def _swiglu_half_step(x, g, wg_ref, wu_ref, wd_ref):
    xn = _rms(x, g).astype(BF16)
    ff = wg_ref.shape[1]
    acts = []
    for lo in range(0, ff, MXU_COLS):
        gate = _dot(xn, wg_ref[:, lo:lo + MXU_COLS])
        up = _dot(xn, wu_ref[:, lo:lo + MXU_COLS])
        acts.append((gate * _sigmoid(gate) * up).astype(BF16))
    act = jnp.concatenate(acts, axis=1)
    return x + 0.5 * _dot(act, wd_ref[...])


def _cast_plan(weights, steps):
    in_specs, out_specs, out_shape = [], [], []
    for w in weights:
        rows, cols = w.shape
        n = max(k for k in range(1, steps + 1)
                if rows % k == 0 and (rows // k) % BF16_ROWS == 0)
        spec = pl.BlockSpec((rows // n, cols), lambda i, n=n: (jnp.minimum(i, n - 1), 0))
        in_specs.append(spec)
        out_specs.append(spec)
        out_shape.append(jax.ShapeDtypeStruct((rows, cols), BF16))
    return in_specs, out_specs, out_shape


def _cast_slabs(src_refs, dst_refs):
    for src, dst in zip(src_refs, dst_refs):
        dst[...] = src[...].astype(BF16)


def _rope_kernel(pos_lo_ref, pos_hi_ref, freq_ref, *refs):
    n_cast = (len(refs) - 2) // 2
    cos_ref, sin_ref = refs[n_cast:n_cast + 2]
    lane = lax.broadcasted_iota(jnp.int32, cos_ref.shape, 1)
    pos = jnp.where(lane < HEAD_DIM // 2, pos_lo_ref[...], pos_hi_ref[...])
    ang = pos.astype(F32) * freq_ref[...]
    cos_ref[...] = jnp.cos(ang)
    sin_ref[...] = jnp.sin(ang)
    _cast_slabs(refs[:n_cast], refs[n_cast + 2:])


def _rope_tables(positions, tm, cast_weights):
    t = positions.size
    half_t = t // 2
    inv_freq = ROPE_THETA ** (-jnp.arange(0, HEAD_DIM, 2, dtype=F32) / HEAD_DIM)
    freq = jnp.concatenate([inv_freq, inv_freq])[None, :]
    pos = positions.reshape(2, half_t, 1)
    tile = pl.BlockSpec((tm, HEAD_DIM), lambda i: (i, 0))
    col = pl.BlockSpec((tm, 1), lambda i: (i, 0))
    steps = half_t // tm
    c_in, c_out, c_shape = _cast_plan(cast_weights, steps)
    outs = pl.pallas_call(
        _rope_kernel,
        grid=(steps,),
        in_specs=[col, col, pl.BlockSpec((1, HEAD_DIM), lambda i: (0, 0))] + c_in,
        out_specs=[tile, tile] + c_out,
        out_shape=[jax.ShapeDtypeStruct((half_t, HEAD_DIM), F32)] * 2 + c_shape,
        compiler_params=_cparams(1),
        name="rope_tables",
    )(pos[0], pos[1], freq, *cast_weights)
    return outs[0], outs[1], outs[2:]


def _unpack_rope(cos_ref, sin_ref, upper_half):
    half = HEAD_DIM // 2
    lane = lax.broadcasted_iota(jnp.int32, cos_ref.shape, 1)
    low = lane < half
    own = jnp.logical_xor(low, upper_half)
    c, s = cos_ref[...], sin_ref[...]
    cos = jnp.where(own, c, pltpu.roll(c, half, axis=1))
    sin = jnp.where(own, s, pltpu.roll(s, half, axis=1))
    return cos, jnp.where(low, -sin, sin)


def _ffn1_kernel(x_ref, g1_ref, wg_ref, wu_ref, wd_ref, gmix_ref, *refs):
    n_cast = (len(refs) - 2) // 2
    h_ref, u_ref = refs[n_cast:n_cast + 2]
    n = x_ref.shape[0] // FFN1_PARTS
    for part in range(FFN1_PARTS):
        rows = slice(part * n, (part + 1) * n)
        h = _swiglu_half_step(x_ref[rows, :], g1_ref[...], wg_ref, wu_ref, wd_ref)
        h_ref[rows, :] = h
        u_ref[rows, :] = _rms(h, gmix_ref[...]).astype(BF16)
    _cast_slabs(refs[:n_cast], refs[n_cast + 2:])


def _ffn1(x, g1, wg, wu, wd, gmix, tm, cast_weights):
    t, d = x.shape
    ff = wg.shape[1]
    tile = pl.BlockSpec((tm, d), lambda i: (i, 0))
    steps = t // tm
    c_in, c_out, c_shape = _cast_plan(cast_weights, steps)
    outs = pl.pallas_call(
        _ffn1_kernel,
        grid=(steps,),
        in_specs=[tile, _resident((1, d)), _resident((d, ff)), _resident((d, ff)),
                  _resident((ff, d)), _resident((1, d))] + c_in,
        out_specs=[tile, tile] + c_out,
        out_shape=[jax.ShapeDtypeStruct((t, d), F32), jax.ShapeDtypeStruct((t, d), BF16)] + c_shape,
        compiler_params=_cparams(1),
        name="ffn1",
    )(x, g1, wg, wu, wd, gmix, *cast_weights)
    return outs[0], outs[1], outs[2:]


def _rotary(t, cos, sin_signed):
    return t * cos + pltpu.roll(t, HEAD_DIM // 2, axis=1) * sin_signed


CONV_OFF = CONV_HALO - (CONV_WIDTH - 1)
CONV_SLABS = 4
CONV_MXU_PHASES = (2, 3, 4, 5, 6, 7)


def _build_conv_diag(w_ref, dmat):
    eye = (lax.broadcasted_iota(jnp.int32, (LANES, LANES), 0)
           == lax.broadcasted_iota(jnp.int32, (LANES, LANES), 1))
    for c in range(dmat.shape[0]):
        for q in range(CONV_SLABS):
            for g, r in enumerate(CONV_MXU_PHASES):
                j = SUBLANES * q + r - CONV_OFF
                assert 0 <= j < CONV_WIDTH
                tap = w_ref[j:j + 1, c * LANES:(c + 1) * LANES]
                dmat[c, q * LANES:(q + 1) * LANES, g * LANES:(g + 1) * LANES] = (
                    jnp.where(eye, tap, 0.0).astype(BF16))


def _conv_lane_chunk(xbuf, w_ref, b_ref, dmat, acc_ref, tm, c):
    cols = slice(c * LANES, (c + 1) * LANES)
    taps = [w_ref[j:j + 1, cols] for j in range(CONV_WIDTH)]
    bias = b_ref[:, cols]
    m_rows = tm + BF16_ROWS
    slabs = [xbuf[SUBLANES * q:SUBLANES * q + m_rows, cols].astype(BF16) for q in range(CONV_SLABS)]
    phase_sums = _dot(jnp.concatenate(slabs, axis=1), dmat[c])
    for rb in range(tm // CONV_ROWS):
        base = rb * CONV_ROWS
        out = None
        for r in range(SUBLANES):
            if r in CONV_MXU_PHASES:
                g = CONV_MXU_PHASES.index(r)
                shifted = phase_sums[base + r:base + r + CONV_ROWS, g * LANES:(g + 1) * LANES]
            else:
                n = CONV_ROWS if r == 0 else CONV_ROWS + SUBLANES
                part = None
                for q in range((CONV_OFF + CONV_WIDTH - 1) // SUBLANES + 1):
                    j = SUBLANES * q + r - CONV_OFF
                    if 0 <= j < CONV_WIDTH:
                        lo = base + SUBLANES * q
                        term = taps[j] * xbuf[lo:lo + n, cols]
                        part = term if part is None else part + term
                shifted = part if r == 0 else part[r:r + CONV_ROWS]
            out = shifted if out is None else out + shifted
        acc_ref[base:base + CONV_ROWS, cols] = out + bias


def _mixer_in_kernel(u_ref, w_ref, cos_ref, sin_ref, wdw_ref, bdw_ref, gln_ref, bln_ref, wco_ref,
                     gyc_ref, q0_ref, q1_ref, q2_ref, k0_ref, k1_ref, k2_ref,
                     v0_ref, v1_ref, v2_ref, gb_ref, xbuf, acc_ref, ga_buf, stage, dmat,
                     *, tm, c_conv, d_model):
    def proj(lo, width):
        return _dot(u_ref[...], w_ref[:, lo:lo + width])

    @pl.when((pl.program_id(0) == 0) & (pl.program_id(1) == 0))
    def _():
        _build_conv_diag(wdw_ref, dmat)
        xbuf[CONV_HALO + tm:, :] = jnp.zeros((BF16_ROWS, c_conv), F32)

    @pl.when(pl.program_id(1) == 0)
    def _():
        xbuf[0:CONV_HALO, :] = jnp.zeros((CONV_HALO, c_conv), F32)

    a = proj(0, c_conv)
    gate = proj(c_conv, c_conv)
    xbuf[CONV_HALO:CONV_HALO + tm, :] = a * _sigmoid(gate)

    def emit(out_ref, val, dil):
        if dil == 1:
            out_ref[...] = val.astype(BF16)
            return
        rows = tm // dil
        for c in range(GROUP_W // LANES):
            stage[c] = val[:, c * LANES:(c + 1) * LANES]
        for r in range(dil):
            for c in range(GROUP_W // LANES):
                lo = r * GROUP_W + c * LANES
                out_ref[:, lo:lo + LANES] = stage[c, pl.ds(r, rows, stride=dil), :].astype(BF16)

    base = 2 * c_conv
    gbase = base + 3 * ATTN_W

    tile_idx = pl.program_id(0) * pl.num_programs(1) + pl.program_id(1)
    n_tiles = pl.num_programs(0) * pl.num_programs(1)
    cos, sin = _unpack_rope(cos_ref, sin_ref, 2 * tile_idx >= n_tiles)

    def rotary_piece(out_ref, lo, dil, scale):
        def piece():
            t = proj(lo, GROUP_W)
            heads = [_rotary(t[:, h * HEAD_DIM:(h + 1) * HEAD_DIM], cos, sin)
                     for h in range(HEADS_PER_GROUP)]
            if scale is not None:
                heads = [x * scale for x in heads]
            emit(out_ref, jnp.concatenate(heads, axis=1), dil)
        return piece

    def value_piece(out_ref, lo, dil):
        return lambda: emit(out_ref, proj(lo, GROUP_W), dil)

    def gate_a_piece():
        ga_buf[...] = _sigmoid(proj(gbase, d_model))

    def gate_b_piece():
        gb_ref[...] = _sigmoid(proj(gbase + d_model, d_model)).astype(BF16)

    pieces = [gate_a_piece, gate_b_piece]
    for g, (q_ref, k_ref, v_ref) in enumerate(
            ((q0_ref, k0_ref, v0_ref), (q1_ref, k1_ref, v1_ref), (q2_ref, k2_ref, v2_ref))):
        dil = GROUPS[g][1]
        pieces.append(rotary_piece(q_ref, base + g * GROUP_W, dil, HEAD_DIM ** -0.5))
        pieces.append(rotary_piece(k_ref, base + ATTN_W + g * GROUP_W, dil, None))
        pieces.append(value_piece(v_ref, base + 2 * ATTN_W + g * GROUP_W, dil))

    for c in range(c_conv // LANES):
        _conv_lane_chunk(xbuf, wdw_ref, bdw_ref, dmat, acc_ref, tm, c)
    xbuf[0:CONV_HALO, :] = xbuf[tm:tm + CONV_HALO, :]
    pieces[0]()

    acc = acc_ref[...]
    mu = jnp.mean(acc, axis=-1, keepdims=True)
    cen = acc - mu
    var = jnp.mean(cen * cen, axis=-1, keepdims=True)
    y = cen * lax.rsqrt(var + EPS) * gln_ref[...] + bln_ref[...]
    y = (y * _sigmoid(y)).astype(BF16)
    gyc_ref[...] = (ga_buf[...] * _dot(y, wco_ref[...])).astype(BF16)

    for piece in pieces[1:]:
        piece()


def _mixer_in(u, w_in, cos, sin, w_dw, b_dw, g_ln, b_ln, w_co, batch, tm):
    t, d = u.shape
    in_w = w_in.shape[1]
    c_conv = w_dw.shape[1]
    per_seq = t // batch // tm
    idx = lambda b, i: (b * per_seq + i, 0)
    row = lambda w: pl.BlockSpec((tm, w), idx)
    half_tiles = t // 2 // tm
    rope = pl.BlockSpec((tm, HEAD_DIM), lambda b, i: (lax.rem(b * per_seq + i, half_tiles), 0))
    strided = lambda dil: pl.BlockSpec((tm // dil, dil * GROUP_W), idx)
    strided_sds = lambda dil: jax.ShapeDtypeStruct((t // dil, dil * GROUP_W), BF16)
    dils = [dil for _, dil in GROUPS]
    out_specs = [row(d)] + [strided(dil) for _ in range(3) for dil in dils] + [row(d)]
    out_shape = ([jax.ShapeDtypeStruct((t, d), BF16)]
                 + [strided_sds(dil) for _ in range(3) for dil in dils]
                 + [jax.ShapeDtypeStruct((t, d), BF16)])
    return pl.pallas_call(
        functools.partial(_mixer_in_kernel, tm=tm, c_conv=c_conv, d_model=d),
        grid=(batch, per_seq),
        in_specs=[row(d), _resident((d, in_w)), rope, rope,
                  _resident((CONV_WIDTH, c_conv)), _resident((1, c_conv)), _resident((1, c_conv)),
                  _resident((1, c_conv)), _resident((c_conv, d))],
        out_specs=out_specs,
        out_shape=out_shape,
        scratch_shapes=[pltpu.VMEM((CONV_HALO + tm + BF16_ROWS, c_conv), F32),
                        pltpu.VMEM((tm, c_conv), F32),
                        pltpu.VMEM((tm, d), F32),
                        pltpu.VMEM((GROUP_W // LANES, tm, LANES), F32),
                        pltpu.VMEM((c_conv // LANES, CONV_SLABS * LANES,
                                    len(CONV_MXU_PHASES) * LANES), BF16)],
        compiler_params=_cparams(2),
        name="mixer_in",
    )(u, w_in, cos, sin, w_dw, b_dw, g_ln, b_ln, w_co)


def _attn_kernel(q_ref, k_ref, v_ref, kh_ref, vh_ref, o_ref, lse_ref, kbuf, vbuf,
                 *, lt, unroll, rps):
    first_tile = pl.program_id(2) == 0
    ones = jnp.ones((lt + Q_BLOCK, HEAD_DIM), BF16)
    for rr in range(rps):
        kbuf[rr, 0:Q_BLOCK, :] = kh_ref[0, :, rr * GROUP_W:(rr + 1) * GROUP_W]
        kbuf[rr, Q_BLOCK:, :] = k_ref[0, :, rr * GROUP_W:(rr + 1) * GROUP_W]
        for h in range(HEADS_PER_GROUP):
            src = slice(rr * GROUP_W + h * HEAD_DIM, rr * GROUP_W + (h + 1) * HEAD_DIM)
            vbuf[rr, 0:Q_BLOCK, 2 * h * HEAD_DIM:(2 * h + 1) * HEAD_DIM] = vh_ref[0, :, src]
            vbuf[rr, Q_BLOCK:, 2 * h * HEAD_DIM:(2 * h + 1) * HEAD_DIM] = v_ref[0, :, src]
            vbuf[rr, :, (2 * h + 1) * HEAD_DIM:(2 * h + 2) * HEAD_DIM] = ones

    qi = lax.broadcasted_iota(jnp.int32, (Q_BLOCK, 2 * Q_BLOCK), 0)
    ki = lax.broadcasted_iota(jnp.int32, (Q_BLOCK, 2 * Q_BLOCK), 1)
    dist = qi + Q_BLOCK - ki
    band = (dist >= 0) & (dist <= Q_BLOCK)
    lane = lax.broadcasted_iota(jnp.int32, (Q_BLOCK, LANES), 1)
    nt = (((1,), (1,)), ((), ()))

    def blocks(units):
        work = [(rr, start, mask, h) for rr, start, mask in units for h in range(HEADS_PER_GROUP)]
        col = lambda rr, h: slice(rr * GROUP_W + h * HEAD_DIM, rr * GROUP_W + (h + 1) * HEAD_DIM)
        s = [lax.dot_general(q_ref[0, pl.ds(start, Q_BLOCK), col(rr, h)],
                             kbuf[rr, pl.ds(start, 2 * Q_BLOCK), col(0, h)], nt,
                             preferred_element_type=F32) for rr, start, _, h in work]
        s = [jnp.where(mask, x, NEG) for x, (_, _, mask, _) in zip(s, work)]
        m = [jnp.max(x, axis=-1, keepdims=True) for x in s]
        e = [jnp.exp(x - mx).astype(BF16) for x, mx in zip(s, m)]
        pv = [_dot(ex, vbuf[rr, pl.ds(start, 2 * Q_BLOCK), 2 * h * HEAD_DIM:(2 * h + 2) * HEAD_DIM])
              for ex, (rr, start, _, h) in zip(e, work)]
        for u, (rr, start, _) in enumerate(units):
            rows = pl.ds(start, Q_BLOCK)
            packed = None
            for h in range(HEADS_PER_GROUP):
                i = u * HEADS_PER_GROUP + h
                den = pv[i][:, HEAD_DIM:]
                o_ref[0, rows, col(rr, h)] = (pv[i][:, :HEAD_DIM] / den).astype(BF16)
                lse = m[i] + jnp.log(den)
                packed = lse if packed is None else jnp.where(lane >= h * STAT_REP, lse, packed)
            lse_ref[0, rows, rr * LANES:(rr + 1) * LANES] = packed

    def run(units):
        for lo in range(0, len(units), unroll):
            blocks(units[lo:lo + unroll])

    per = lt // Q_BLOCK
    upr = min(unroll, per)
    first_mask = band & jnp.logical_or(ki >= Q_BLOCK, jnp.logical_not(first_tile))
    run([(rr, j * Q_BLOCK, first_mask if j == 0 else band)
         for rr in range(rps) for j in range(upr)])

    def body(n, carry):
        start = n * (upr * Q_BLOCK)
        run([(rr, pl.multiple_of(start + j * Q_BLOCK, Q_BLOCK), band)
             for rr in range(rps) for j in range(upr)])
        return carry

    lax.fori_loop(1, per // upr, body, 0)


def _attention_group(q, k, v, batch, dilation, lt, rps):
    rows, _ = q.shape
    l = rows // batch
    lt = min(lt, l)
    per = lt // Q_BLOCK
    unroll = min(ATTN_UNROLL, per * rps)
    assert l % lt == 0 and dilation % rps == 0 and per % min(unroll, per) == 0
    view = lambda x: x.reshape(batch, l, x.shape[-1])
    tile = lambda width: pl.BlockSpec((1, lt, rps * width), lambda bi, r, i: (bi, i, r))
    halo = pl.BlockSpec((1, Q_BLOCK, rps * GROUP_W),
                        lambda bi, r, i: (bi, jnp.maximum(i * per - 1, 0), r))
    o, lse = pl.pallas_call(
        functools.partial(_attn_kernel, lt=lt, unroll=unroll, rps=rps),
        grid=(batch, dilation // rps, l // lt),
        in_specs=[tile(GROUP_W), tile(GROUP_W), tile(GROUP_W), halo, halo],
        out_specs=[tile(GROUP_W), tile(LANES)],
        out_shape=[jax.ShapeDtypeStruct((batch, l, dilation * GROUP_W), BF16),
                   jax.ShapeDtypeStruct((batch, l, dilation * LANES), F32)],
        scratch_shapes=[pltpu.VMEM((rps, lt + Q_BLOCK, GROUP_W), BF16),
                        pltpu.VMEM((rps, lt + Q_BLOCK, 2 * GROUP_W), BF16)],
        compiler_params=_cparams(3),
        name=f"attn_d{dilation}",
    )(view(q), view(k), view(v), view(k), view(v))
    return o.reshape(rows, dilation * GROUP_W), lse.reshape(rows, dilation * LANES)


def _tail_kernel(h_ref, gyc_ref, gb_ref, o0_ref, o1_ref, o2_ref, l0_ref, l1_ref, l2_ref, p_ref,
                 wao_ref, wmix_ref, g2_ref, wg_ref, wu_ref, wd_ref, gple_ref, wpg_ref, wpp_ref,
                 gfin_ref, y_ref, obuf, lbuf, *, tm, final_norm):
    parts = [_mixer_out_rows(part, tm // TAIL_PARTS, h_ref, gyc_ref, gb_ref,
                             (o0_ref, o1_ref, o2_ref), (l0_ref, l1_ref, l2_ref),
                             wao_ref, wmix_ref, obuf, lbuf) for part in range(TAIL_PARTS)]
    h = parts[0] if TAIL_PARTS == 1 else jnp.concatenate(parts, axis=0)
    h = _swiglu_half_step(h, g2_ref[...], wg_ref, wu_ref, wd_ref)
    gate = _sigmoid(_dot(_rms(h, gple_ref[...]).astype(BF16), wpg_ref[...]))
    h = h + gate * _dot(p_ref[...].astype(BF16), wpp_ref[...])
    y_ref[...] = _rms(h, gfin_ref[...]) if final_norm else h


def _mixer_out_rows(part, tp, h_ref, gyc_ref, gb_ref, o_refs, l_refs, wao_ref, wmix_ref,
                    obuf, lbuf):
    rows_tok = slice(part * tp, (part + 1) * tp)

    def token_order(dst, src_ref, dil, width):
        chunks = width // LANES
        n = tp // dil
        src_rows = slice(part * n, (part + 1) * n)
        if dil == 1:
            return [src_ref[src_rows, c * LANES:(c + 1) * LANES].astype(F32) for c in range(chunks)]
        for r in range(dil):
            for c in range(chunks):
                lo = r * width + c * LANES
                dst[c, pl.ds(part * tp + r, n, stride=dil), :] = (
                    src_ref[src_rows, lo:lo + LANES].astype(F32))
        return [dst[c, rows_tok, :] for c in range(chunks)]

    dils = [dil for _, dil in GROUPS]
    ls = [token_order(lbuf.at[g], ref, dils[g], LANES)[0] for g, ref in enumerate(l_refs)]
    os_ = [token_order(obuf.at[g], ref, dils[g], GROUP_W) for g, ref in enumerate(o_refs)]

    m = jnp.maximum(jnp.maximum(ls[0], ls[1]), ls[2])
    es = [jnp.exp(x - m) for x in ls]
    inv = 1.0 / (es[0] + es[1] + es[2])
    alphas = [x * inv for x in es]
    src = lax.broadcasted_iota(jnp.int32, (2 * LANES, GROUP_W), 0) % LANES
    dst_head = lax.broadcasted_iota(jnp.int32, (2 * LANES, GROUP_W), 1) // HEAD_DIM
    spread = jnp.where(src == dst_head * STAT_REP, 1.0, 0.0).astype(BF16)
    wide = []
    for a in alphas:
        hi = a.astype(BF16)
        lo = (a - hi.astype(F32)).astype(BF16)
        wide.append(_dot(jnp.concatenate([hi, lo], axis=1), spread))
    heads = []
    for h in range(HEADS_PER_GROUP):
        acc = None
        for g in range(N_GROUPS):
            term = wide[g][:, h * HEAD_DIM:(h + 1) * HEAD_DIM] * os_[g][h]
            acc = term if acc is None else acc + term
        heads.append(acc.astype(BF16))
    o = jnp.concatenate(heads, axis=1)
    y_attn = _dot(o, wao_ref[...])
    mixed = gyc_ref[rows_tok, :].astype(F32) + gb_ref[rows_tok, :].astype(F32) * y_attn
    return h_ref[rows_tok, :] + _dot(mixed.astype(BF16), wmix_ref[...])


def _tail(h, gyc, gb, os_, ls_, p, wao, wmix, g2, wg, wu, wd, gple, wpg, wpp, gfin, tm,
          final_norm):
    t, d = h.shape
    row = lambda w: pl.BlockSpec((tm, w), lambda i: (i, 0))
    strided = lambda dil, w: pl.BlockSpec((tm // dil, dil * w), lambda i: (i, 0))
    res = lambda x: _resident(x.shape)
    weights = (wao, wmix, g2, wg, wu, wd, gple, wpg, wpp, gfin)
    dils = [dil for _, dil in GROUPS]
    return pl.pallas_call(
        functools.partial(_tail_kernel, tm=tm, final_norm=final_norm),
        grid=(t // tm,),
        in_specs=[row(d), row(d), row(d)] + [strided(dil, GROUP_W) for dil in dils]
                 + [strided(dil, LANES) for dil in dils]
                 + [row(p.shape[1])] + [res(x) for x in weights],
        out_specs=row(d),
        out_shape=jax.ShapeDtypeStruct((t, d), F32),
        scratch_shapes=[pltpu.VMEM((N_GROUPS, GROUP_W // LANES, tm, LANES), F32),
                        pltpu.VMEM((N_GROUPS, 1, tm, LANES), F32)],
        compiler_params=_cparams(1),
        name="tail",
    )(h, gyc, gb, *os_, *ls_, p, *weights)


def kernel(x, p, positions, g_ffn1, w_ffn1_gate, w_ffn1_up, w_ffn1_down, g_mix, w_in,
           w_dw, b_dw, g_conv_ln, b_conv_ln, w_conv_out, w_attn_out, w_mix_out,
           g_ffn2, w_ffn2_gate, w_ffn2_up, w_ffn2_down, g_ple, w_ple_gate, w_ple_proj,
           g_final):
    b, s, d = x.shape
    depth = p.shape[0]
    t = b * s
    c_conv = w_dw.shape[-1]
    assert w_in.shape[-1] == 2 * c_conv + 3 * ATTN_W + 2 * d
    tm, tm_mix = 512, 256
    max_dil = max(dil for _, dil in GROUPS)
    assert s % tm == 0 and s % tm_mix == 0
    assert tm_mix % CONV_ROWS == 0 and (tm_mix // max_dil) % 16 == 0
    for window, dil in GROUPS:
        assert window // dil == Q_BLOCK and (s // dil) % Q_BLOCK == 0

    assert depth == 1
    i = 0
    row = lambda v: v.reshape(1, -1)
    cos, sin, (wg1, wu1, wd1) = _rope_tables(
        positions, tm, [w_ffn1_gate[i], w_ffn1_up[i], w_ffn1_down[i]])
    h, u, (win, wco, wao, wmix, wg2, wu2, wd2, wpg, wpp) = _ffn1(
        x.reshape(t, d), row(g_ffn1[i]), wg1, wu1, wd1, row(g_mix[i]), tm,
        [w_in[i], w_conv_out[i], w_attn_out[i], w_mix_out[i], w_ffn2_gate[i], w_ffn2_up[i],
         w_ffn2_down[i], w_ple_gate[i], w_ple_proj[i]])
    gyc, q0, q1, q2, k0, k1, k2, v0, v1, v2, gb = _mixer_in(
        u, win, cos, sin, w_dw[i], row(b_dw[i]), row(g_conv_ln[i]), row(b_conv_ln[i]), wco,
        b, tm_mix)
    os_, ls_ = [], []
    for (_, dil), q, k, v in zip(GROUPS, (q0, q1, q2), (k0, k1, k2), (v0, v1, v2)):
        o, lse = _attention_group(q, k, v, b, dil, ATTN_ROWS, max(1, ATTN_ROWS * dil // s))
        os_.append(o)
        ls_.append(lse)
    h = _tail(h, gyc, gb, os_, ls_, p[i].reshape(t, -1), wao, wmix, row(g_ffn2[i]), wg2, wu2,
              wd2, row(g_ple[i]), wpg, wpp, row(g_final), tm, True)
    return h.reshape(b, s, d)
```

```python
import functools

import jax
import jax.numpy as jnp
from jax import lax
from jax.experimental import pallas as pl
from jax.experimental.pallas import tpu as pltpu

HEAD_DIM = 128
HEADS_PER_GROUP = 4
GROUPS = ((128, 1), (512, 4), (2048, 16))
N_GROUPS = len(GROUPS)
GROUP_W = HEADS_PER_GROUP * HEAD_DIM
ATTN_W = N_GROUPS * GROUP_W
CONV_WIDTH = 31
Q_BLOCK = 128
ROPE_THETA = 10000.0
EPS = 1e-6
NEG = -1e30

LANES = 128
SUBLANES = 8
BF16_ROWS = 16
MXU_COLS = 256
CONV_HALO = 32
CONV_ROWS = 128
ATTN_UNROLL = 4
ATTN_ROWS = 2048
ROPE_ROWS = 1024
FFN1_PARTS = 2
TAIL_PARTS = 1
STAT_REP = LANES // HEADS_PER_GROUP
VMEM_LIMIT = 56 * 1024 * 1024

BF16 = jnp.bfloat16
F32 = jnp.float32


def _cparams(n_axes):
    return pltpu.CompilerParams(
        dimension_semantics=("arbitrary",) * n_axes, vmem_limit_bytes=VMEM_LIMIT)


def _resident(shape):
    nd = len(shape)
    return pl.BlockSpec(shape, lambda *_: (0,) * nd, pipeline_mode=pl.Buffered(1))


def _rms(x, g):
    ms = jnp.mean(x * x, axis=-1, keepdims=True)
    return x * lax.rsqrt(ms + EPS) * g


def _sigmoid(x):
    return 1.0 / (1.0 + jnp.exp(-x))


def _dot(a, b):
    return jnp.dot(a, b, preferred_element_type=F32)


def _swiglu_half_step(x, g, wg_ref, wu_ref, wd_ref):
    xn = _rms(x, g).astype(BF16)
    ff = wg_ref.shape[1]
    acts = []
    for lo in range(0, ff, MXU_COLS):
        gate = _dot(xn, wg_ref[:, lo:lo + MXU_COLS])
        up = _dot(xn, wu_ref[:, lo:lo + MXU_COLS])
        acts.append((gate * _sigmoid(gate) * up).astype(BF16))
    act = jnp.concatenate(acts, axis=1)
    return x + 0.5 * _dot(act, wd_ref[...])


def _cast_plan(weights, steps):
    in_specs, out_specs, out_shape = [], [], []
    for w in weights:
        rows, cols = w.shape
        n = max(k for k in range(1, steps + 1)
                if rows % k == 0 and (rows // k) % BF16_ROWS == 0)
        spec = pl.BlockSpec((rows // n, cols), lambda i, n=n: (jnp.minimum(i, n - 1), 0))
        in_specs.append(spec)
        out_specs.append(spec)
        out_shape.append(jax.ShapeDtypeStruct((rows, cols), BF16))
    return in_specs, out_specs, out_shape


def _cast_slabs(src_refs, dst_refs):
    for src, dst in zip(src_refs, dst_refs):
        dst[...] = src[...].astype(BF16)


def _rope_kernel(pos_lo_ref, pos_hi_ref, freq_ref, *refs):
    n_cast = (len(refs) - 2) // 2
    cos_ref, sin_ref = refs[n_cast:n_cast + 2]
    lane = lax.broadcasted_iota(jnp.int32, cos_ref.shape, 1)
    pos = jnp.where(lane < HEAD_DIM // 2, pos_lo_ref[...], pos_hi_ref[...])
    ang = pos.astype(F32) * freq_ref[...]
    cos_ref[...] = jnp.cos(ang)
    sin_ref[...] = jnp.sin(ang)
    _cast_slabs(refs[:n_cast], refs[n_cast + 2:])


def _rope_tables(positions, tm, cast_weights):
    t = positions.size
    half_t = t // 2
    inv_freq = ROPE_THETA ** (-jnp.arange(0, HEAD_DIM, 2, dtype=F32) / HEAD_DIM)
    freq = jnp.concatenate([inv_freq, inv_freq])[None, :]
    pos = positions.reshape(2, half_t, 1)
    tile = pl.BlockSpec((tm, HEAD_DIM), lambda i: (i, 0))
    col = pl.BlockSpec((tm, 1), lambda i: (i, 0))
    steps = half_t // tm
    c_in, c_out, c_shape = _cast_plan(cast_weights, steps)
    outs = pl.pallas_call(
        _rope_kernel,
        grid=(steps,),
        in_specs=[col, col, pl.BlockSpec((1, HEAD_DIM), lambda i: (0, 0))] + c_in,
        out_specs=[tile, tile] + c_out,
        out_shape=[jax.ShapeDtypeStruct((half_t, HEAD_DIM), F32)] * 2 + c_shape,
        compiler_params=_cparams(1),
        name="rope_tables",
    )(pos[0], pos[1], freq, *cast_weights)
    return outs[0], outs[1], outs[2:]


def _unpack_rope(cos_ref, sin_ref, upper_half):
    half = HEAD_DIM // 2
    lane = lax.broadcasted_iota(jnp.int32, cos_ref.shape, 1)
    low = lane < half
    own = jnp.logical_xor(low, upper_half)
    c, s = cos_ref[...], sin_ref[...]
    cos = jnp.where(own, c, pltpu.roll(c, half, axis=1))
    sin = jnp.where(own, s, pltpu.roll(s, half, axis=1))
    return cos, jnp.where(low, -sin, sin)


def _ffn1_kernel(x_ref, g1_ref, wg_ref, wu_ref, wd_ref, gmix_ref, *refs):
    n_cast = (len(refs) - 2) // 2
    h_ref, u_ref = refs[n_cast:n_cast + 2]
    n = x_ref.shape[0] // FFN1_PARTS
    for part in range(FFN1_PARTS):
        rows = slice(part * n, (part + 1) * n)
        h = _swiglu_half_step(x_ref[rows, :], g1_ref[...], wg_ref, wu_ref, wd_ref)
        h_ref[rows, :] = h
        u_ref[rows, :] = _rms(h, gmix_ref[...]).astype(BF16)
    _cast_slabs(refs[:n_cast], refs[n_cast + 2:])


def _ffn1(x, g1, wg, wu, wd, gmix, tm, cast_weights):
    t, d = x.shape
    ff = wg.shape[1]
    tile = pl.BlockSpec((tm, d), lambda i: (i, 0))
    steps = t // tm
    c_in, c_out, c_shape = _cast_plan(cast_weights, steps)
    outs = pl.pallas_call(
        _ffn1_kernel,
        grid=(steps,),
        in_specs=[tile, _resident((1, d)), _resident((d, ff)), _resident((d, ff)),
                  _resident((ff, d)), _resident((1, d))] + c_in,
        out_specs=[tile, tile] + c_out,
        out_shape=[jax.ShapeDtypeStruct((t, d), F32), jax.ShapeDtypeStruct((t, d), BF16)] + c_shape,
        compiler_params=_cparams(1),
        name="ffn1",
    )(x, g1, wg, wu, wd, gmix, *cast_weights)
    return outs[0], outs[1], outs[2:]


def _rotary(t, cos, sin_signed):
    return t * cos + pltpu.roll(t, HEAD_DIM // 2, axis=1) * sin_signed


CONV_OFF = CONV_HALO - (CONV_WIDTH - 1)
CONV_SLABS = 4
CONV_MXU_PHASES = (2, 3, 4, 5, 6, 7)


def _build_conv_diag(w_ref, dmat):
    eye = (lax.broadcasted_iota(jnp.int32, (LANES, LANES), 0)
           == lax.broadcasted_iota(jnp.int32, (LANES, LANES), 1))
    for c in range(dmat.shape[0]):
        for q in range(CONV_SLABS):
            for g, r in enumerate(CONV_MXU_PHASES):
                j = SUBLANES * q + r - CONV_OFF
                assert 0 <= j < CONV_WIDTH
                tap = w_ref[j:j + 1, c * LANES:(c + 1) * LANES]
                dmat[c, q * LANES:(q + 1) * LANES, g * LANES:(g + 1) * LANES] = (
                    jnp.where(eye, tap, 0.0).astype(BF16))


def _conv_lane_chunk(xbuf, w_ref, b_ref, dmat, acc_ref, tm, c):
    cols = slice(c * LANES, (c + 1) * LANES)
    taps = [w_ref[j:j + 1, cols] for j in range(CONV_WIDTH)]
    bias = b_ref[:, cols]
    m_rows = tm + BF16_ROWS
    slabs = [xbuf[SUBLANES * q:SUBLANES * q + m_rows, cols].astype(BF16) for q in range(CONV_SLABS)]
    phase_sums = _dot(jnp.concatenate(slabs, axis=1), dmat[c])
    for rb in range(tm // CONV_ROWS):
        base = rb * CONV_ROWS
        out = None
        for r in range(SUBLANES):
            if r in CONV_MXU_PHASES:
                g = CONV_MXU_PHASES.index(r)
                shifted = phase_sums[base + r:base + r + CONV_ROWS, g * LANES:(g + 1) * LANES]
            else:
                n = CONV_ROWS if r == 0 else CONV_ROWS + SUBLANES
                part = None
                for q in range((CONV_OFF + CONV_WIDTH - 1) // SUBLANES + 1):
                    j = SUBLANES * q + r - CONV_OFF
                    if 0 <= j < CONV_WIDTH:
                        lo = base + SUBLANES * q
                        term = taps[j] * xbuf[lo:lo + n, cols]
                        part = term if part is None else part + term
                shifted = part if r == 0 else part[r:r + CONV_ROWS]
            out = shifted if out is None else out + shifted
        acc_ref[base:base + CONV_ROWS, cols] = out + bias


def _mixer_in_kernel(u_ref, w_ref, cos_ref, sin_ref, wdw_ref, bdw_ref, gln_ref, bln_ref, wco_ref,
                     gyc_ref, q0_ref, q1_ref, q2_ref, k0_ref, k1_ref, k2_ref,
                     v0_ref, v1_ref, v2_ref, gb_ref, xbuf, acc_ref, ga_buf, rope_buf, perm, dmat,
                     *, tm, c_conv, d_model):
    dils = [dil for _, dil in GROUPS]

    @pl.when((pl.program_id(0) == 0) & (pl.program_id(1) == 0))
    def _():
        _build_conv_diag(wdw_ref, dmat)
        xbuf[CONV_HALO + tm:, :] = jnp.zeros((BF16_ROWS, c_conv), F32)
        dst = lax.broadcasted_iota(jnp.int32, (tm, tm), 0)
        tok = lax.broadcasted_iota(jnp.int32, (tm, tm), 1)
        for g, dil in enumerate(dils):
            if dil > 1:
                n = tm // dil
                perm[g] = jnp.where(tok == (dst % n) * dil + dst // n, 1.0, 0.0).astype(BF16)

    @pl.when(pl.program_id(1) == 0)
    def _():
        xbuf[0:CONV_HALO, :] = jnp.zeros((CONV_HALO, c_conv), F32)

    u = u_ref[...]

    def proj(lo, width, lhs=None):
        return _dot(u if lhs is None else lhs, w_ref[:, lo:lo + width])

    a = proj(0, c_conv)
    gate = proj(c_conv, c_conv)
    xbuf[CONV_HALO:CONV_HALO + tm, :] = a * _sigmoid(gate)

    base = 2 * c_conv
    gbase = base + 3 * ATTN_W

    tile_idx = pl.program_id(0) * pl.num_programs(1) + pl.program_id(1)
    n_tiles = pl.num_programs(0) * pl.num_programs(1)
    cos, sin = _unpack_rope(cos_ref, sin_ref, 2 * tile_idx >= n_tiles)
    rope_buf[0] = cos
    rope_buf[1] = sin

    def strided_rows(g):
        dil = dils[g]
        if dil == 1:
            return u, cos, sin
        n = tm // dil
        u_g = _dot(perm[g], u).astype(BF16)
        cos_g = jnp.concatenate([rope_buf[0, pl.ds(r, n, stride=dil), :] for r in range(dil)], axis=0)
        sin_g = jnp.concatenate([rope_buf[1, pl.ds(r, n, stride=dil), :] for r in range(dil)], axis=0)
        return u_g, cos_g, sin_g

    def emit(out_ref, val, dil):
        n = tm // dil
        for r in range(dil):
            out_ref[:, r * GROUP_W:(r + 1) * GROUP_W] = val[r * n:(r + 1) * n, :].astype(BF16)

    def group_pieces(g, q_ref, k_ref, v_ref):
        dil = dils[g]
        cache = []

        def inputs():
            if not cache:
                cache.append(strided_rows(g))
            return cache[0]

        def rotary_piece(out_ref, lo, scale):
            def piece():
                u_g, cos_g, sin_g = inputs()
                t = proj(lo, GROUP_W, u_g)
                heads = [_rotary(t[:, h * HEAD_DIM:(h + 1) * HEAD_DIM], cos_g, sin_g)
                         for h in range(HEADS_PER_GROUP)]
                if scale is not None:
                    heads = [x * scale for x in heads]
                emit(out_ref, jnp.concatenate(heads, axis=1), dil)
            return piece

        def value_piece():
            emit(v_ref, proj(base + 2 * ATTN_W + g * GROUP_W, GROUP_W, inputs()[0]), dil)

        return [rotary_piece(q_ref, base + g * GROUP_W, HEAD_DIM ** -0.5),
                rotary_piece(k_ref, base + ATTN_W + g * GROUP_W, None), value_piece]

    def gate_a_piece():
        ga_buf[...] = _sigmoid(proj(gbase, d_model))

    def gate_b_piece():
        gb_ref[...] = _sigmoid(proj(gbase + d_model, d_model)).astype(BF16)

    pieces = [gate_a_piece, gate_b_piece]
    for g, refs in enumerate(((q0_ref, k0_ref, v0_ref), (q1_ref, k1_ref, v1_ref),
                              (q2_ref, k2_ref, v2_ref))):
        pieces += group_pieces(g, *refs)

    for c in range(c_conv // LANES):
        _conv_lane_chunk(xbuf, wdw_ref, bdw_ref, dmat, acc_ref, tm, c)
    xbuf[0:CONV_HALO, :] = xbuf[tm:tm + CONV_HALO, :]
    pieces[0]()

    acc = acc_ref[...]
    mu = jnp.mean(acc, axis=-1, keepdims=True)
    cen = acc - mu
    var = jnp.mean(cen * cen, axis=-1, keepdims=True)
    y = cen * lax.rsqrt(var + EPS) * gln_ref[...] + bln_ref[...]
    y = (y * _sigmoid(y)).astype(BF16)
    gyc_ref[...] = (ga_buf[...] * _dot(y, wco_ref[...])).astype(BF16)

    for piece in pieces[1:]:
        piece()


def _mixer_in(u, w_in, cos, sin, w_dw, b_dw, g_ln, b_ln, w_co, batch, tm):
    t, d = u.shape
    in_w = w_in.shape[1]
    c_conv = w_dw.shape[1]
    per_seq = t // batch // tm
    idx = lambda b, i: (b * per_seq + i, 0)
    row = lambda w: pl.BlockSpec((tm, w), idx)
    half_tiles = t // 2 // tm
    rope = pl.BlockSpec((tm, HEAD_DIM), lambda b, i: (lax.rem(b * per_seq + i, half_tiles), 0))
    strided = lambda dil: pl.BlockSpec((tm // dil, dil * GROUP_W), idx)
    strided_sds = lambda dil: jax.ShapeDtypeStruct((t // dil, dil * GROUP_W), BF16)
    dils = [dil for _, dil in GROUPS]
    out_specs = [row(d)] + [strided(dil) for _ in range(3) for dil in dils] + [row(d)]
    out_shape = ([jax.ShapeDtypeStruct((t, d), BF16)]
                 + [strided_sds(dil) for _ in range(3) for dil in dils]
                 + [jax.ShapeDtypeStruct((t, d), BF16)])
    return pl.pallas_call(
        functools.partial(_mixer_in_kernel, tm=tm, c_conv=c_conv, d_model=d),
        grid=(batch, per_seq),
        in_specs=[row(d), _resident((d, in_w)), rope, rope,
                  _resident((CONV_WIDTH, c_conv)), _resident((1, c_conv)), _resident((1, c_conv)),
                  _resident((1, c_conv)), _resident((c_conv, d))],
        out_specs=out_specs,
        out_shape=out_shape,
        scratch_shapes=[pltpu.VMEM((CONV_HALO + tm + BF16_ROWS, c_conv), F32),
                        pltpu.VMEM((tm, c_conv), F32),
                        pltpu.VMEM((tm, d), F32),
                        pltpu.VMEM((2, tm, HEAD_DIM), F32),
                        pltpu.VMEM((N_GROUPS, tm, tm), BF16),
                        pltpu.VMEM((c_conv // LANES, CONV_SLABS * LANES,
                                    len(CONV_MXU_PHASES) * LANES), BF16)],
        compiler_params=_cparams(2),
        name="mixer_in",
    )(u, w_in, cos, sin, w_dw, b_dw, g_ln, b_ln, w_co)


def _attn_kernel(q_ref, k_ref, v_ref, kh_ref, vh_ref, o_ref, lse_ref, kbuf, vbuf,
                 *, lt, unroll, rps):
    first_tile = pl.program_id(2) == 0
    ones = jnp.ones((lt + Q_BLOCK, HEAD_DIM), BF16)
    for rr in range(rps):
        kbuf[rr, 0:Q_BLOCK, :] = kh_ref[0, :, rr * GROUP_W:(rr + 1) * GROUP_W]
        kbuf[rr, Q_BLOCK:, :] = k_ref[0, :, rr * GROUP_W:(rr + 1) * GROUP_W]
        for h in range(HEADS_PER_GROUP):
            src = slice(rr * GROUP_W + h * HEAD_DIM, rr * GROUP_W + (h + 1) * HEAD_DIM)
            vbuf[rr, 0:Q_BLOCK, 2 * h * HEAD_DIM:(2 * h + 1) * HEAD_DIM] = vh_ref[0, :, src]
            vbuf[rr, Q_BLOCK:, 2 * h * HEAD_DIM:(2 * h + 1) * HEAD_DIM] = v_ref[0, :, src]
            vbuf[rr, :, (2 * h + 1) * HEAD_DIM:(2 * h + 2) * HEAD_DIM] = ones

    qi = lax.broadcasted_iota(jnp.int32, (Q_BLOCK, 2 * Q_BLOCK), 0)
    ki = lax.broadcasted_iota(jnp.int32, (Q_BLOCK, 2 * Q_BLOCK), 1)
    dist = qi + Q_BLOCK - ki
    band = (dist >= 0) & (dist <= Q_BLOCK)
    lane = lax.broadcasted_iota(jnp.int32, (Q_BLOCK, LANES), 1)
    den_lane = lane % STAT_REP >= STAT_REP // 2
    nt = (((1,), (1,)), ((), ()))

    def blocks(units):
        work = [(rr, start, mask, h) for rr, start, mask in units for h in range(HEADS_PER_GROUP)]
        col = lambda rr, h: slice(rr * GROUP_W + h * HEAD_DIM, rr * GROUP_W + (h + 1) * HEAD_DIM)
        s = [lax.dot_general(q_ref[0, pl.ds(start, Q_BLOCK), col(rr, h)],
                             kbuf[rr, pl.ds(start, 2 * Q_BLOCK), col(0, h)], nt,
                             preferred_element_type=F32) for rr, start, _, h in work]
        s = [jnp.where(mask, x, NEG) for x, (_, _, mask, _) in zip(s, work)]
        m = [jnp.max(x, axis=-1, keepdims=True) for x in s]
        e = [jnp.exp(x - mx).astype(BF16) for x, mx in zip(s, m)]
        pv = [_dot(ex, vbuf[rr, pl.ds(start, 2 * Q_BLOCK), 2 * h * HEAD_DIM:(2 * h + 2) * HEAD_DIM])
              for ex, (rr, start, _, h) in zip(e, work)]
        for u, (rr, start, _) in enumerate(units):
            rows = pl.ds(start, Q_BLOCK)
            packed = None
            for h in range(HEADS_PER_GROUP):
                i = u * HEADS_PER_GROUP + h
                den = pv[i][:, HEAD_DIM:]
                o_ref[0, rows, col(rr, h)] = pv[i][:, :HEAD_DIM].astype(BF16)
                stat = jnp.where(den_lane, den, m[i])
                packed = stat if packed is None else jnp.where(lane >= h * STAT_REP, stat, packed)
            lse_ref[0, rows, rr * LANES:(rr + 1) * LANES] = packed

    def run(units):
        for lo in range(0, len(units), unroll):
            blocks(units[lo:lo + unroll])

    per = lt // Q_BLOCK
    upr = min(unroll, per)
    first_mask = band & jnp.logical_or(ki >= Q_BLOCK, jnp.logical_not(first_tile))
    run([(rr, j * Q_BLOCK, first_mask if j == 0 else band)
         for rr in range(rps) for j in range(upr)])

    def body(n, carry):
        start = n * (upr * Q_BLOCK)
        run([(rr, pl.multiple_of(start + j * Q_BLOCK, Q_BLOCK), band)
             for rr in range(rps) for j in range(upr)])
        return carry

    lax.fori_loop(1, per // upr, body, 0)


def _attention_group(q, k, v, batch, dilation, lt, rps):
    rows, _ = q.shape
    l = rows // batch
    lt = min(lt, l)
    per = lt // Q_BLOCK
    unroll = min(ATTN_UNROLL, per * rps)
    assert l % lt == 0 and dilation % rps == 0 and per % min(unroll, per) == 0
    view = lambda x: x.reshape(batch, l, x.shape[-1])
    tile = lambda width: pl.BlockSpec((1, lt, rps * width), lambda bi, r, i: (bi, i, r))
    halo = pl.BlockSpec((1, Q_BLOCK, rps * GROUP_W),
                        lambda bi, r, i: (bi, jnp.maximum(i * per - 1, 0), r))
    o, lse = pl.pallas_call(
        functools.partial(_attn_kernel, lt=lt, unroll=unroll, rps=rps),
        grid=(batch, dilation // rps, l // lt),
        in_specs=[tile(GROUP_W), tile(GROUP_W), tile(GROUP_W), halo, halo],
        out_specs=[tile(GROUP_W), tile(LANES)],
        out_shape=[jax.ShapeDtypeStruct((batch, l, dilation * GROUP_W), BF16),
                   jax.ShapeDtypeStruct((batch, l, dilation * LANES), F32)],
        scratch_shapes=[pltpu.VMEM((rps, lt + Q_BLOCK, GROUP_W), BF16),
                        pltpu.VMEM((rps, lt + Q_BLOCK, 2 * GROUP_W), BF16)],
        compiler_params=_cparams(3),
        name=f"attn_d{dilation}",
    )(view(q), view(k), view(v), view(k), view(v))
    return o.reshape(rows, dilation * GROUP_W), lse.reshape(rows, dilation * LANES)


def _tail_kernel(h_ref, gyc_ref, gb_ref, o0_ref, o1_ref, o2_ref, l0_ref, l1_ref, l2_ref, p_ref,
                 wao_ref, wmix_ref, g2_ref, wg_ref, wu_ref, wd_ref, gple_ref, wpg_ref, wpp_ref,
                 gfin_ref, y_ref, obuf, lbuf, *, tm, final_norm):
    parts = [_mixer_out_rows(part, tm // TAIL_PARTS, h_ref, gyc_ref, gb_ref,
                             (o0_ref, o1_ref, o2_ref), (l0_ref, l1_ref, l2_ref),
                             wao_ref, wmix_ref, obuf, lbuf) for part in range(TAIL_PARTS)]
    h = parts[0] if TAIL_PARTS == 1 else jnp.concatenate(parts, axis=0)
    h = _swiglu_half_step(h, g2_ref[...], wg_ref, wu_ref, wd_ref)
    gate = _sigmoid(_dot(_rms(h, gple_ref[...]).astype(BF16), wpg_ref[...]))
    h = h + gate * _dot(p_ref[...].astype(BF16), wpp_ref[...])
    y_ref[...] = _rms(h, gfin_ref[...]) if final_norm else h


def _mixer_out_rows(part, tp, h_ref, gyc_ref, gb_ref, o_refs, l_refs, wao_ref, wmix_ref,
                    obuf, lbuf):
    rows_tok = slice(part * tp, (part + 1) * tp)

    def token_order(dst, src_ref, dil, width):
        chunks = width // LANES
        n = tp // dil
        src_rows = slice(part * n, (part + 1) * n)
        if dil == 1:
            return [src_ref[src_rows, c * LANES:(c + 1) * LANES].astype(F32) for c in range(chunks)]
        for r in range(dil):
            for c in range(chunks):
                lo = r * width + c * LANES
                dst[c, pl.ds(part * tp + r, n, stride=dil), :] = (
                    src_ref[src_rows, lo:lo + LANES].astype(F32))
        return [dst[c, rows_tok, :] for c in range(chunks)]

    dils = [dil for _, dil in GROUPS]
    ls = [token_order(lbuf.at[g], ref, dils[g], LANES)[0] for g, ref in enumerate(l_refs)]
    os_ = [token_order(obuf.at[g], ref, dils[g], GROUP_W) for g, ref in enumerate(o_refs)]

    half = STAT_REP // 2
    lane = lax.broadcasted_iota(jnp.int32, ls[0].shape, 1)
    m = jnp.maximum(jnp.maximum(ls[0], ls[1]), ls[2])
    es = [jnp.exp(x - m) for x in ls]
    dens = [pltpu.roll(x, LANES - half, axis=1) for x in ls]
    inv = 1.0 / (es[0] * dens[0] + es[1] * dens[1] + es[2] * dens[2])
    alphas = [jnp.where(lane % STAT_REP < half, x * inv, 0.0) for x in es]
    src = lax.broadcasted_iota(jnp.int32, (2 * LANES, GROUP_W), 0) % LANES
    dst_head = lax.broadcasted_iota(jnp.int32, (2 * LANES, GROUP_W), 1) // HEAD_DIM
    spread = jnp.where(src == dst_head * STAT_REP, 1.0, 0.0).astype(BF16)
    wide = []
    for a in alphas:
        hi = a.astype(BF16)
        lo = (a - hi.astype(F32)).astype(BF16)
        wide.append(_dot(jnp.concatenate([hi, lo], axis=1), spread))
    heads = []
    for h in range(HEADS_PER_GROUP):
        acc = None
        for g in range(N_GROUPS):
            term = wide[g][:, h * HEAD_DIM:(h + 1) * HEAD_DIM] * os_[g][h]
            acc = term if acc is None else acc + term
        heads.append(acc.astype(BF16))
    o = jnp.concatenate(heads, axis=1)
    y_attn = _dot(o, wao_ref[...])
    mixed = gyc_ref[rows_tok, :].astype(F32) + gb_ref[rows_tok, :].astype(F32) * y_attn
    return h_ref[rows_tok, :] + _dot(mixed.astype(BF16), wmix_ref[...])


def _tail(h, gyc, gb, os_, ls_, p, wao, wmix, g2, wg, wu, wd, gple, wpg, wpp, gfin, tm,
          final_norm):
    t, d = h.shape
    row = lambda w: pl.BlockSpec((tm, w), lambda i: (i, 0))
    strided = lambda dil, w: pl.BlockSpec((tm // dil, dil * w), lambda i: (i, 0))
    res = lambda x: _resident(x.shape)
    weights = (wao, wmix, g2, wg, wu, wd, gple, wpg, wpp, gfin)
    dils = [dil for _, dil in GROUPS]
    return pl.pallas_call(
        functools.partial(_tail_kernel, tm=tm, final_norm=final_norm),
        grid=(t // tm,),
        in_specs=[row(d), row(d), row(d)] + [strided(dil, GROUP_W) for dil in dils]
                 + [strided(dil, LANES) for dil in dils]
                 + [row(p.shape[1])] + [res(x) for x in weights],
        out_specs=row(d),
        out_shape=jax.ShapeDtypeStruct((t, d), F32),
        scratch_shapes=[pltpu.VMEM((N_GROUPS, GROUP_W // LANES, tm, LANES), F32),
                        pltpu.VMEM((N_GROUPS, 1, tm, LANES), F32)],
        compiler_params=_cparams(1),
        name="tail",
    )(h, gyc, gb, *os_, *ls_, p, *weights)


def kernel(x, p, positions, g_ffn1, w_ffn1_gate, w_ffn1_up, w_ffn1_down, g_mix, w_in,
           w_dw, b_dw, g_conv_ln, b_conv_ln, w_conv_out, w_attn_out, w_mix_out,
           g_ffn2, w_ffn2_gate, w_ffn2_up, w_ffn2_down, g_ple, w_ple_gate, w_ple_proj,
           g_final):
    b, s, d = x.shape
    depth = p.shape[0]
    t = b * s
    c_conv = w_dw.shape[-1]
    assert w_in.shape[-1] == 2 * c_conv + 3 * ATTN_W + 2 * d
    tm, tm_mix = 512, 256
    max_dil = max(dil for _, dil in GROUPS)
    assert s % tm == 0 and s % tm_mix == 0
    assert tm_mix % CONV_ROWS == 0 and (tm_mix // max_dil) % 16 == 0
    for window, dil in GROUPS:
        assert window // dil == Q_BLOCK and (s // dil) % Q_BLOCK == 0

    assert depth == 1
    i = 0
    row = lambda v: v.reshape(1, -1)
    cos, sin, (wg1, wu1, wd1) = _rope_tables(
        positions, ROPE_ROWS, [w_ffn1_gate[i], w_ffn1_up[i], w_ffn1_down[i]])
    h, u, (win, wco, wao, wmix, wg2, wu2, wd2, wpg, wpp) = _ffn1(
        x.reshape(t, d), row(g_ffn1[i]), wg1, wu1, wd1, row(g_mix[i]), tm,
        [w_in[i], w_conv_out[i], w_attn_out[i], w_mix_out[i], w_ffn2_gate[i], w_ffn2_up[i],
         w_ffn2_down[i], w_ple_gate[i], w_ple_proj[i]])
    gyc, q0, q1, q2, k0, k1, k2, v0, v1, v2, gb = _mixer_in(
        u, win, cos, sin, w_dw[i], row(b_dw[i]), row(g_conv_ln[i]), row(b_conv_ln[i]), wco,
        b, tm_mix)
    os_, ls_ = [], []
    for (_, dil), q, k, v in zip(GROUPS, (q0, q1, q2), (k0, k1, k2), (v0, v1, v2)):
        o, lse = _attention_group(q, k, v, b, dil, ATTN_ROWS, max(1, ATTN_ROWS * dil // s))
        os_.append(o)
        ls_.append(lse)
    h = _tail(h, gyc, gb, os_, ls_, p[i].reshape(t, -1), wao, wmix, row(g_ffn2[i]), wg2, wu2,
              wd2, row(g_ple[i]), wpg, wpp, row(g_final), tm, True)
    return h.reshape(b, s, d)
```

```python
import functools

import jax
import jax.numpy as jnp
from jax import lax
from jax.experimental import pallas as pl
from jax.experimental.pallas import tpu as pltpu

HEAD_DIM = 128
HEADS_PER_GROUP = 4
GROUPS = ((128, 1), (512, 4), (2048, 16))
N_GROUPS = len(GROUPS)
GROUP_W = HEADS_PER_GROUP * HEAD_DIM
ATTN_W = N_GROUPS * GROUP_W
CONV_WIDTH = 31
Q_BLOCK = 128
ROPE_THETA = 10000.0
EPS = 1e-6
NEG = -1e30

LANES = 128
SUBLANES = 8
BF16_ROWS = 16
MXU_COLS = 256
CONV_HALO = 32
CONV_ROWS = 128
ATTN_UNROLL = 4
ATTN_ROWS = 2048
ROPE_ROWS = 1024
FFN1_PARTS = 2
TAIL_PARTS = 1
STAT_REP = LANES // HEADS_PER_GROUP
VMEM_LIMIT = 56 * 1024 * 1024

BF16 = jnp.bfloat16
F32 = jnp.float32


def _cparams(n_axes):
    return pltpu.CompilerParams(
        dimension_semantics=("arbitrary",) * n_axes, vmem_limit_bytes=VMEM_LIMIT)


def _resident(shape):
    nd = len(shape)
    return pl.BlockSpec(shape, lambda *_: (0,) * nd, pipeline_mode=pl.Buffered(1))


def _rms(x, g):
    ms = jnp.mean(x * x, axis=-1, keepdims=True)
    return x * lax.rsqrt(ms + EPS) * g


def _sigmoid(x):
    return 1.0 / (1.0 + jnp.exp(-x))


def _dot(a, b):
    return jnp.dot(a, b, preferred_element_type=F32)


def _swiglu_half_step(x, g, wg_ref, wu_ref, wd_ref):
    xn = _rms(x, g).astype(BF16)
    ff = wg_ref.shape[1]
    acts = []
    for lo in range(0, ff, MXU_COLS):
        gate = _dot(xn, wg_ref[:, lo:lo + MXU_COLS])
        up = _dot(xn, wu_ref[:, lo:lo + MXU_COLS])
        acts.append((gate * _sigmoid(gate) * up).astype(BF16))
    act = jnp.concatenate(acts, axis=1)
    return x + 0.5 * _dot(act, wd_ref[...])


def _cast_plan(weights, steps):
    in_specs, out_specs, out_shape = [], [], []
    for w in weights:
        rows, cols = w.shape
        n = max(k for k in range(1, steps + 1)
                if rows % k == 0 and (rows // k) % BF16_ROWS == 0)
        spec = pl.BlockSpec((rows // n, cols), lambda i, n=n: (jnp.minimum(i, n - 1), 0))
        in_specs.append(spec)
        out_specs.append(spec)
        out_shape.append(jax.ShapeDtypeStruct((rows, cols), BF16))
    return in_specs, out_specs, out_shape


def _cast_slabs(src_refs, dst_refs):
    for src, dst in zip(src_refs, dst_refs):
        dst[...] = src[...].astype(BF16)


def _rope_kernel(pos_lo_ref, pos_hi_ref, freq_ref, *refs):
    n_cast = (len(refs) - 2) // 2
    cos_ref, sin_ref = refs[n_cast:n_cast + 2]
    lane = lax.broadcasted_iota(jnp.int32, cos_ref.shape, 1)
    pos = jnp.where(lane < HEAD_DIM // 2, pos_lo_ref[...], pos_hi_ref[...])
    ang = pos.astype(F32) * freq_ref[...]
    cos_ref[...] = jnp.cos(ang)
    sin_ref[...] = jnp.sin(ang)
    _cast_slabs(refs[:n_cast], refs[n_cast + 2:])


def _rope_tables(positions, tm, cast_weights):
    t = positions.size
    half_t = t // 2
    inv_freq = ROPE_THETA ** (-jnp.arange(0, HEAD_DIM, 2, dtype=F32) / HEAD_DIM)
    freq = jnp.concatenate([inv_freq, inv_freq])[None, :]
    pos = positions.reshape(2, half_t, 1)
    tile = pl.BlockSpec((tm, HEAD_DIM), lambda i: (i, 0))
    col = pl.BlockSpec((tm, 1), lambda i: (i, 0))
    steps = half_t // tm
    c_in, c_out, c_shape = _cast_plan(cast_weights, steps)
    outs = pl.pallas_call(
        _rope_kernel,
        grid=(steps,),
        in_specs=[col, col, pl.BlockSpec((1, HEAD_DIM), lambda i: (0, 0))] + c_in,
        out_specs=[tile, tile] + c_out,
        out_shape=[jax.ShapeDtypeStruct((half_t, HEAD_DIM), F32)] * 2 + c_shape,
        compiler_params=_cparams(1),
        name="rope_tables",
    )(pos[0], pos[1], freq, *cast_weights)
    return outs[0], outs[1], outs[2:]


def _unpack_rope(cos_ref, sin_ref, upper_half):
    half = HEAD_DIM // 2
    lane = lax.broadcasted_iota(jnp.int32, cos_ref.shape, 1)
    low = lane < half
    own = jnp.logical_xor(low, upper_half)
    c, s = cos_ref[...], sin_ref[...]
    cos = jnp.where(own, c, pltpu.roll(c, half, axis=1))
    sin = jnp.where(own, s, pltpu.roll(s, half, axis=1))
    return cos, jnp.where(low, -sin, sin)


def _ffn1_kernel(x_ref, g1_ref, wg_ref, wu_ref, wd_ref, gmix_ref, *refs):
    n_cast = (len(refs) - 2) // 2
    h_ref, u_ref = refs[n_cast:n_cast + 2]
    n = x_ref.shape[0] // FFN1_PARTS
    for part in range(FFN1_PARTS):
        rows = slice(part * n, (part + 1) * n)
        h = _swiglu_half_step(x_ref[rows, :], g1_ref[...], wg_ref, wu_ref, wd_ref)
        h_ref[rows, :] = h
        u_ref[rows, :] = _rms(h, gmix_ref[...]).astype(BF16)
    _cast_slabs(refs[:n_cast], refs[n_cast + 2:])


def _ffn1(x, g1, wg, wu, wd, gmix, tm, cast_weights):
    t, d = x.shape
    ff = wg.shape[1]
    tile = pl.BlockSpec((tm, d), lambda i: (i, 0))
    steps = t // tm
    c_in, c_out, c_shape = _cast_plan(cast_weights, steps)
    outs = pl.pallas_call(
        _ffn1_kernel,
        grid=(steps,),
        in_specs=[tile, _resident((1, d)), _resident((d, ff)), _resident((d, ff)),
                  _resident((ff, d)), _resident((1, d))] + c_in,
        out_specs=[tile, tile] + c_out,
        out_shape=[jax.ShapeDtypeStruct((t, d), F32), jax.ShapeDtypeStruct((t, d), BF16)] + c_shape,
        compiler_params=_cparams(1),
        name="ffn1",
    )(x, g1, wg, wu, wd, gmix, *cast_weights)
    return outs[0], outs[1], outs[2:]


def _rotary(t, cos, sin_signed):
    return t * cos + pltpu.roll(t, HEAD_DIM // 2, axis=1) * sin_signed


CONV_OFF = CONV_HALO - (CONV_WIDTH - 1)
CONV_SLABS = 4
CONV_MXU_PHASES = (2, 3, 4, 5, 6, 7)


def _build_conv_diag(w_ref, dmat):
    eye = (lax.broadcasted_iota(jnp.int32, (LANES, LANES), 0)
           == lax.broadcasted_iota(jnp.int32, (LANES, LANES), 1))
    for c in range(dmat.shape[0]):
        for q in range(CONV_SLABS):
            for g, r in enumerate(CONV_MXU_PHASES):
                j = SUBLANES * q + r - CONV_OFF
                assert 0 <= j < CONV_WIDTH
                tap = w_ref[j:j + 1, c * LANES:(c + 1) * LANES]
                dmat[c, q * LANES:(q + 1) * LANES, g * LANES:(g + 1) * LANES] = (
                    jnp.where(eye, tap, 0.0).astype(BF16))


def _conv_lane_chunk(xbuf, w_ref, b_ref, dmat, acc_ref, tm, c):
    cols = slice(c * LANES, (c + 1) * LANES)
    taps = [w_ref[j:j + 1, cols] for j in range(CONV_WIDTH)]
    bias = b_ref[:, cols]
    m_rows = tm + BF16_ROWS
    slabs = [xbuf[SUBLANES * q:SUBLANES * q + m_rows, cols].astype(BF16) for q in range(CONV_SLABS)]
    phase_sums = _dot(jnp.concatenate(slabs, axis=1), dmat[c])
    for rb in range(tm // CONV_ROWS):
        base = rb * CONV_ROWS
        out = None
        for r in range(SUBLANES):
            if r in CONV_MXU_PHASES:
                g = CONV_MXU_PHASES.index(r)
                shifted = phase_sums[base + r:base + r + CONV_ROWS, g * LANES:(g + 1) * LANES]
            else:
                n = CONV_ROWS if r == 0 else CONV_ROWS + SUBLANES
                part = None
                for q in range((CONV_OFF + CONV_WIDTH - 1) // SUBLANES + 1):
                    j = SUBLANES * q + r - CONV_OFF
                    if 0 <= j < CONV_WIDTH:
                        lo = base + SUBLANES * q
                        term = taps[j] * xbuf[lo:lo + n, cols]
                        part = term if part is None else part + term
                shifted = part if r == 0 else part[r:r + CONV_ROWS]
            out = shifted if out is None else out + shifted
        acc_ref[base:base + CONV_ROWS, cols] = out + bias


def _mixer_in_kernel(u_ref, w_ref, cos_ref, sin_ref, wdw_ref, bdw_ref, gln_ref, bln_ref, wco_ref,
                     gyc_ref, q0_ref, q1_ref, q2_ref, k0_ref, k1_ref, k2_ref,
                     v0_ref, v1_ref, v2_ref, gb_ref, xbuf, acc_ref, ga_buf, rope_buf, perm, dmat,
                     *, tm, c_conv, d_model):
    dils = [dil for _, dil in GROUPS]

    @pl.when((pl.program_id(0) == 0) & (pl.program_id(1) == 0))
    def _():
        _build_conv_diag(wdw_ref, dmat)
        xbuf[CONV_HALO + tm:, :] = jnp.zeros((BF16_ROWS, c_conv), F32)
        dst = lax.broadcasted_iota(jnp.int32, (tm, tm), 0)
        tok = lax.broadcasted_iota(jnp.int32, (tm, tm), 1)
        for g, dil in enumerate(dils):
            if dil > 1:
                n = tm // dil
                perm[g] = jnp.where(tok == (dst % n) * dil + dst // n, 1.0, 0.0).astype(BF16)

    @pl.when(pl.program_id(1) == 0)
    def _():
        xbuf[0:CONV_HALO, :] = jnp.zeros((CONV_HALO, c_conv), F32)

    u = u_ref[...]

    def proj(lo, width, lhs=None):
        return _dot(u if lhs is None else lhs, w_ref[:, lo:lo + width])

    a = proj(0, c_conv)
    gate = proj(c_conv, c_conv)
    xbuf[CONV_HALO:CONV_HALO + tm, :] = a * _sigmoid(gate)

    base = 2 * c_conv
    gbase = base + 3 * ATTN_W

    tile_idx = pl.program_id(0) * pl.num_programs(1) + pl.program_id(1)
    n_tiles = pl.num_programs(0) * pl.num_programs(1)
    cos, sin = _unpack_rope(cos_ref, sin_ref, 2 * tile_idx >= n_tiles)
    rope_buf[0] = cos
    rope_buf[1] = sin

    def strided_rows(g):
        dil = dils[g]
        if dil == 1:
            return u, cos, sin
        n = tm // dil
        u_g = _dot(perm[g], u).astype(BF16)
        cos_g = jnp.concatenate([rope_buf[0, pl.ds(r, n, stride=dil), :] for r in range(dil)], axis=0)
        sin_g = jnp.concatenate([rope_buf[1, pl.ds(r, n, stride=dil), :] for r in range(dil)], axis=0)
        return u_g, cos_g, sin_g

    def emit(out_ref, val, dil):
        n = tm // dil
        for r in range(dil):
            out_ref[:, r * GROUP_W:(r + 1) * GROUP_W] = val[r * n:(r + 1) * n, :].astype(BF16)

    def group_pieces(g, q_ref, k_ref, v_ref):
        dil = dils[g]
        cache = []

        def inputs():
            if not cache:
                cache.append(strided_rows(g))
            return cache[0]

        def rotary_piece(out_ref, lo, scale):
            def piece():
                u_g, cos_g, sin_g = inputs()
                t = proj(lo, GROUP_W, u_g)
                heads = [_rotary(t[:, h * HEAD_DIM:(h + 1) * HEAD_DIM], cos_g, sin_g)
                         for h in range(HEADS_PER_GROUP)]
                if scale is not None:
                    heads = [x * scale for x in heads]
                emit(out_ref, jnp.concatenate(heads, axis=1), dil)
            return piece

        def value_piece():
            emit(v_ref, proj(base + 2 * ATTN_W + g * GROUP_W, GROUP_W, inputs()[0]), dil)

        return [rotary_piece(q_ref, base + g * GROUP_W, HEAD_DIM ** -0.5),
                rotary_piece(k_ref, base + ATTN_W + g * GROUP_W, None), value_piece]

    def gate_a_piece():
        ga_buf[...] = _sigmoid(proj(gbase, d_model))

    def gate_b_piece():
        gb_ref[...] = _sigmoid(proj(gbase + d_model, d_model)).astype(BF16)

    pieces = [gate_a_piece, gate_b_piece]
    for g, refs in enumerate(((q0_ref, k0_ref, v0_ref), (q1_ref, k1_ref, v1_ref),
                              (q2_ref, k2_ref, v2_ref))):
        pieces += group_pieces(g, *refs)

    for c in range(c_conv // LANES):
        _conv_lane_chunk(xbuf, wdw_ref, bdw_ref, dmat, acc_ref, tm, c)
    xbuf[0:CONV_HALO, :] = xbuf[tm:tm + CONV_HALO, :]
    pieces[0]()

    acc = acc_ref[...]
    mu = jnp.mean(acc, axis=-1, keepdims=True)
    cen = acc - mu
    var = jnp.mean(cen * cen, axis=-1, keepdims=True)
    y = cen * lax.rsqrt(var + EPS) * gln_ref[...] + bln_ref[...]
    y = (y * _sigmoid(y)).astype(BF16)
    gyc_ref[...] = (ga_buf[...] * _dot(y, wco_ref[...])).astype(BF16)

    for piece in pieces[1:]:
        piece()


def _mixer_in(u, w_in, cos, sin, w_dw, b_dw, g_ln, b_ln, w_co, batch, tm):
    t, d = u.shape
    in_w = w_in.shape[1]
    c_conv = w_dw.shape[1]
    per_seq = t // batch // tm
    idx = lambda b, i: (b * per_seq + i, 0)
    row = lambda w: pl.BlockSpec((tm, w), idx)
    half_tiles = t // 2 // tm
    rope = pl.BlockSpec((tm, HEAD_DIM), lambda b, i: (lax.rem(b * per_seq + i, half_tiles), 0))
    strided = lambda dil: pl.BlockSpec((tm // dil, dil * GROUP_W), idx)
    strided_sds = lambda dil: jax.ShapeDtypeStruct((t // dil, dil * GROUP_W), BF16)
    dils = [dil for _, dil in GROUPS]
    out_specs = [row(d)] + [strided(dil) for _ in range(3) for dil in dils] + [row(d)]
    out_shape = ([jax.ShapeDtypeStruct((t, d), BF16)]
                 + [strided_sds(dil) for _ in range(3) for dil in dils]
                 + [jax.ShapeDtypeStruct((t, d), BF16)])
    return pl.pallas_call(
        functools.partial(_mixer_in_kernel, tm=tm, c_conv=c_conv, d_model=d),
        grid=(batch, per_seq),
        in_specs=[row(d), _resident((d, in_w)), rope, rope,
                  _resident((CONV_WIDTH, c_conv)), _resident((1, c_conv)), _resident((1, c_conv)),
                  _resident((1, c_conv)), _resident((c_conv, d))],
        out_specs=out_specs,
        out_shape=out_shape,
        scratch_shapes=[pltpu.VMEM((CONV_HALO + tm + BF16_ROWS, c_conv), F32),
                        pltpu.VMEM((tm, c_conv), F32),
                        pltpu.VMEM((tm, d), F32),
                        pltpu.VMEM((2, tm, HEAD_DIM), F32),
                        pltpu.VMEM((N_GROUPS, tm, tm), BF16),
                        pltpu.VMEM((c_conv // LANES, CONV_SLABS * LANES,
                                    len(CONV_MXU_PHASES) * LANES), BF16)],
        compiler_params=_cparams(2),
        name="mixer_in",
    )(u, w_in, cos, sin, w_dw, b_dw, g_ln, b_ln, w_co)


def _attn_kernel(q_ref, k_ref, v_ref, kh_ref, vh_ref, o_ref, lse_ref, kbuf, vbuf,
                 *, lt, unroll, rps):
    first_tile = pl.program_id(2) == 0
    ones = jnp.ones((lt + Q_BLOCK, HEAD_DIM), BF16)
    for rr in range(rps):
        kbuf[rr, 0:Q_BLOCK, :] = kh_ref[0, :, rr * GROUP_W:(rr + 1) * GROUP_W]
        kbuf[rr, Q_BLOCK:, :] = k_ref[0, :, rr * GROUP_W:(rr + 1) * GROUP_W]
        for h in range(HEADS_PER_GROUP):
            src = slice(rr * GROUP_W + h * HEAD_DIM, rr * GROUP_W + (h + 1) * HEAD_DIM)
            vbuf[rr, 0:Q_BLOCK, 2 * h * HEAD_DIM:(2 * h + 1) * HEAD_DIM] = vh_ref[0, :, src]
            vbuf[rr, Q_BLOCK:, 2 * h * HEAD_DIM:(2 * h + 1) * HEAD_DIM] = v_ref[0, :, src]
            vbuf[rr, :, (2 * h + 1) * HEAD_DIM:(2 * h + 2) * HEAD_DIM] = ones

    qi = lax.broadcasted_iota(jnp.int32, (Q_BLOCK, 2 * Q_BLOCK), 0)
    ki = lax.broadcasted_iota(jnp.int32, (Q_BLOCK, 2 * Q_BLOCK), 1)
    dist = qi + Q_BLOCK - ki
    band = (dist >= 0) & (dist <= Q_BLOCK)
    lane = lax.broadcasted_iota(jnp.int32, (Q_BLOCK, LANES), 1)
    den_lane = lane % STAT_REP >= STAT_REP // 2
    nt = (((1,), (1,)), ((), ()))

    def blocks(units):
        work = [(rr, start, mask, h) for rr, start, mask in units for h in range(HEADS_PER_GROUP)]
        col = lambda rr, h: slice(rr * GROUP_W + h * HEAD_DIM, rr * GROUP_W + (h + 1) * HEAD_DIM)
        s = [lax.dot_general(q_ref[0, pl.ds(start, Q_BLOCK), col(rr, h)],
                             kbuf[rr, pl.ds(start, 2 * Q_BLOCK), col(0, h)], nt,
                             preferred_element_type=F32) for rr, start, _, h in work]
        s = [jnp.where(mask, x, NEG) for x, (_, _, mask, _) in zip(s, work)]
        m = [jnp.max(x, axis=-1, keepdims=True) for x in s]
        e = [jnp.exp(x - mx).astype(BF16) for x, mx in zip(s, m)]
        pv = [_dot(ex, vbuf[rr, pl.ds(start, 2 * Q_BLOCK), 2 * h * HEAD_DIM:(2 * h + 2) * HEAD_DIM])
              for ex, (rr, start, _, h) in zip(e, work)]
        for u, (rr, start, _) in enumerate(units):
            rows = pl.ds(start, Q_BLOCK)
            packed = None
            for h in range(HEADS_PER_GROUP):
                i = u * HEADS_PER_GROUP + h
                den = pv[i][:, HEAD_DIM:]
                o_ref[0, rows, col(rr, h)] = pv[i][:, :HEAD_DIM].astype(BF16)
                stat = jnp.where(den_lane, den, m[i])
                packed = stat if packed is None else jnp.where(lane >= h * STAT_REP, stat, packed)
            lse_ref[0, rows, rr * LANES:(rr + 1) * LANES] = packed

    def run(units):
        for lo in range(0, len(units), unroll):
            blocks(units[lo:lo + unroll])

    per = lt // Q_BLOCK
    upr = min(unroll, per)
    first_mask = band & jnp.logical_or(ki >= Q_BLOCK, jnp.logical_not(first_tile))
    run([(rr, j * Q_BLOCK, first_mask if j == 0 else band)
         for rr in range(rps) for j in range(upr)])

    def body(n, carry):
        start = n * (upr * Q_BLOCK)
        run([(rr, pl.multiple_of(start + j * Q_BLOCK, Q_BLOCK), band)
             for rr in range(rps) for j in range(upr)])
        return carry

    lax.fori_loop(1, per // upr, body, 0)


def _attention_group(q, k, v, batch, dilation, lt, rps):
    rows, _ = q.shape
    l = rows // batch
    lt = min(lt, l)
    per = lt // Q_BLOCK
    unroll = min(ATTN_UNROLL, per * rps)
    assert l % lt == 0 and dilation % rps == 0 and per % min(unroll, per) == 0
    view = lambda x: x.reshape(batch, l, x.shape[-1])
    tile = lambda width: pl.BlockSpec((1, lt, rps * width), lambda bi, r, i: (bi, i, r))
    halo = pl.BlockSpec((1, Q_BLOCK, rps * GROUP_W),
                        lambda bi, r, i: (bi, jnp.maximum(i * per - 1, 0), r))
    o, lse = pl.pallas_call(
        functools.partial(_attn_kernel, lt=lt, unroll=unroll, rps=rps),
        grid=(batch, dilation // rps, l // lt),
        in_specs=[tile(GROUP_W), tile(GROUP_W), tile(GROUP_W), halo, halo],
        out_specs=[tile(GROUP_W), tile(LANES)],
        out_shape=[jax.ShapeDtypeStruct((batch, l, dilation * GROUP_W), BF16),
                   jax.ShapeDtypeStruct((batch, l, dilation * LANES), F32)],
        scratch_shapes=[pltpu.VMEM((rps, lt + Q_BLOCK, GROUP_W), BF16),
                        pltpu.VMEM((rps, lt + Q_BLOCK, 2 * GROUP_W), BF16)],
        compiler_params=_cparams(3),
        name=f"attn_d{dilation}",
    )(view(q), view(k), view(v), view(k), view(v))
    return o.reshape(rows, dilation * GROUP_W), lse.reshape(rows, dilation * LANES)


def _tail_kernel(h_ref, gyc_ref, gb_ref, o0_ref, o1_ref, o2_ref, l0_ref, l1_ref, l2_ref, p_ref,
                 wao_ref, wmix_ref, g2_ref, wg_ref, wu_ref, wd_ref, gple_ref, wpg_ref, wpp_ref,
                 gfin_ref, y_ref, obuf, lbuf, spread, *, tm, final_norm):
    @pl.when(pl.program_id(0) == 0)
    def _():
        src = lax.broadcasted_iota(jnp.int32, spread.shape, 0) % LANES
        dst_head = lax.broadcasted_iota(jnp.int32, spread.shape, 1) // HEAD_DIM
        spread[...] = jnp.where(src == dst_head * STAT_REP, 1.0, 0.0).astype(BF16)

    parts = [_mixer_out_rows(part, tm // TAIL_PARTS, h_ref, gyc_ref, gb_ref,
                             (o0_ref, o1_ref, o2_ref), (l0_ref, l1_ref, l2_ref),
                             wao_ref, wmix_ref, obuf, lbuf, spread) for part in range(TAIL_PARTS)]
    h = parts[0] if TAIL_PARTS == 1 else jnp.concatenate(parts, axis=0)
    h = _swiglu_half_step(h, g2_ref[...], wg_ref, wu_ref, wd_ref)
    gate = _sigmoid(_dot(_rms(h, gple_ref[...]).astype(BF16), wpg_ref[...]))
    h = h + gate * _dot(p_ref[...].astype(BF16), wpp_ref[...])
    y_ref[...] = _rms(h, gfin_ref[...]) if final_norm else h


def _mixer_out_rows(part, tp, h_ref, gyc_ref, gb_ref, o_refs, l_refs, wao_ref, wmix_ref,
                    obuf, lbuf, spread):
    rows_tok = slice(part * tp, (part + 1) * tp)

    def token_order(dst, src_ref, dil, width):
        chunks = width // LANES
        n = tp // dil
        src_rows = slice(part * n, (part + 1) * n)
        if dil == 1:
            return [src_ref[src_rows, c * LANES:(c + 1) * LANES].astype(F32) for c in range(chunks)]
        for r in range(dil):
            for c in range(chunks):
                lo = r * width + c * LANES
                dst[c, pl.ds(part * tp + r, n, stride=dil), :] = (
                    src_ref[src_rows, lo:lo + LANES].astype(F32))
        return [dst[c, rows_tok, :] for c in range(chunks)]

    dils = [dil for _, dil in GROUPS]
    ls = [token_order(lbuf.at[g], ref, dils[g], LANES)[0] for g, ref in enumerate(l_refs)]
    os_ = [token_order(obuf.at[g], ref, dils[g], GROUP_W) for g, ref in enumerate(o_refs)]

    half = STAT_REP // 2
    lane = lax.broadcasted_iota(jnp.int32, ls[0].shape, 1)
    m = jnp.maximum(jnp.maximum(ls[0], ls[1]), ls[2])
    es = [jnp.exp(x - m) for x in ls]
    dens = [pltpu.roll(x, LANES - half, axis=1) for x in ls]
    inv = 1.0 / (es[0] * dens[0] + es[1] * dens[1] + es[2] * dens[2])
    alphas = [jnp.where(lane % STAT_REP < half, x * inv, 0.0) for x in es]
    wide = []
    for a in alphas:
        hi = a.astype(BF16)
        lo = (a - hi.astype(F32)).astype(BF16)
        wide.append(_dot(jnp.concatenate([hi, lo], axis=1), spread[...]))
    heads = []
    for h in range(HEADS_PER_GROUP):
        acc = None
        for g in range(N_GROUPS):
            term = wide[g][:, h * HEAD_DIM:(h + 1) * HEAD_DIM] * os_[g][h]
            acc = term if acc is None else acc + term
        heads.append(acc.astype(BF16))
    o = jnp.concatenate(heads, axis=1)
    y_attn = _dot(o, wao_ref[...])
    mixed = gyc_ref[rows_tok, :].astype(F32) + gb_ref[rows_tok, :].astype(F32) * y_attn
    return h_ref[rows_tok, :] + _dot(mixed.astype(BF16), wmix_ref[...])


def _tail(h, gyc, gb, os_, ls_, p, wao, wmix, g2, wg, wu, wd, gple, wpg, wpp, gfin, tm,
          final_norm):
    t, d = h.shape
    row = lambda w: pl.BlockSpec((tm, w), lambda i: (i, 0))
    strided = lambda dil, w: pl.BlockSpec((tm // dil, dil * w), lambda i: (i, 0))
    res = lambda x: _resident(x.shape)
    weights = (wao, wmix, g2, wg, wu, wd, gple, wpg, wpp, gfin)
    dils = [dil for _, dil in GROUPS]
    return pl.pallas_call(
        functools.partial(_tail_kernel, tm=tm, final_norm=final_norm),
        grid=(t // tm,),
        in_specs=[row(d), row(d), row(d)] + [strided(dil, GROUP_W) for dil in dils]
                 + [strided(dil, LANES) for dil in dils]
                 + [row(p.shape[1])] + [res(x) for x in weights],
        out_specs=row(d),
        out_shape=jax.ShapeDtypeStruct((t, d), F32),
        scratch_shapes=[pltpu.VMEM((N_GROUPS, GROUP_W // LANES, tm, LANES), F32),
                        pltpu.VMEM((N_GROUPS, 1, tm, LANES), F32),
                        pltpu.VMEM((2 * LANES, GROUP_W), BF16)],
        compiler_params=_cparams(1),
        name="tail",
    )(h, gyc, gb, *os_, *ls_, p, *weights)


def kernel(x, p, positions, g_ffn1, w_ffn1_gate, w_ffn1_up, w_ffn1_down, g_mix, w_in,
           w_dw, b_dw, g_conv_ln, b_conv_ln, w_conv_out, w_attn_out, w_mix_out,
           g_ffn2, w_ffn2_gate, w_ffn2_up, w_ffn2_down, g_ple, w_ple_gate, w_ple_proj,
           g_final):
    b, s, d = x.shape
    depth = p.shape[0]
    t = b * s
    c_conv = w_dw.shape[-1]
    assert w_in.shape[-1] == 2 * c_conv + 3 * ATTN_W + 2 * d
    tm, tm_mix = 512, 256
    max_dil = max(dil for _, dil in GROUPS)
    assert s % tm == 0 and s % tm_mix == 0
    assert tm_mix % CONV_ROWS == 0 and (tm_mix // max_dil) % 16 == 0
    for window, dil in GROUPS:
        assert window // dil == Q_BLOCK and (s // dil) % Q_BLOCK == 0

    assert depth == 1
    i = 0
    row = lambda v: v.reshape(1, -1)
    cos, sin, (wg1, wu1, wd1) = _rope_tables(
        positions, ROPE_ROWS, [w_ffn1_gate[i], w_ffn1_up[i], w_ffn1_down[i]])
    h, u, (win, wco, wao, wmix, wg2, wu2, wd2, wpg, wpp) = _ffn1(
        x.reshape(t, d), row(g_ffn1[i]), wg1, wu1, wd1, row(g_mix[i]), tm,
        [w_in[i], w_conv_out[i], w_attn_out[i], w_mix_out[i], w_ffn2_gate[i], w_ffn2_up[i],
         w_ffn2_down[i], w_ple_gate[i], w_ple_proj[i]])
    gyc, q0, q1, q2, k0, k1, k2, v0, v1, v2, gb = _mixer_in(
        u, win, cos, sin, w_dw[i], row(b_dw[i]), row(g_conv_ln[i]), row(b_conv_ln[i]), wco,
        b, tm_mix)
    os_, ls_ = [], []
    for (_, dil), q, k, v in zip(GROUPS, (q0, q1, q2), (k0, k1, k2), (v0, v1, v2)):
        o, lse = _attention_group(q, k, v, b, dil, ATTN_ROWS, max(1, ATTN_ROWS * dil // s))
        os_.append(o)
        ls_.append(lse)
    h = _tail(h, gyc, gb, os_, ls_, p[i].reshape(t, -1), wao, wmix, row(g_ffn2[i]), wg2, wu2,
              wd2, row(g_ple[i]), wpg, wpp, row(g_final), tm, True)
    return h.reshape(b, s, d)
```

```python
import functools

import jax
import jax.numpy as jnp
from jax import lax
from jax.experimental import pallas as pl
from jax.experimental.pallas import tpu as pltpu

HEAD_DIM = 128
HEADS_PER_GROUP = 4
GROUPS = ((128, 1), (512, 4), (2048, 16))
N_GROUPS = len(GROUPS)
GROUP_W = HEADS_PER_GROUP * HEAD_DIM
ATTN_W = N_GROUPS * GROUP_W
CONV_WIDTH = 31
Q_BLOCK = 128
ROPE_THETA = 10000.0
EPS = 1e-6
NEG = -1e30

LANES = 128
SUBLANES = 8
BF16_ROWS = 16
MXU_COLS = 256
CONV_HALO = 32
CONV_ROWS = 128
ATTN_UNROLL = 4
ATTN_ROWS = 2048
ROPE_ROWS = 1024
FFN1_ROWS = 1024
FFN1_PARTS = 4
TAIL_PARTS = 1
STAT_REP = LANES // HEADS_PER_GROUP
VMEM_LIMIT = 56 * 1024 * 1024

BF16 = jnp.bfloat16
F32 = jnp.float32


def _cparams(n_axes):
    return pltpu.CompilerParams(
        dimension_semantics=("arbitrary",) * n_axes, vmem_limit_bytes=VMEM_LIMIT)


def _resident(shape):
    nd = len(shape)
    return pl.BlockSpec(shape, lambda *_: (0,) * nd, pipeline_mode=pl.Buffered(1))


def _rms(x, g):
    ms = jnp.mean(x * x, axis=-1, keepdims=True)
    return x * lax.rsqrt(ms + EPS) * g


def _sigmoid(x):
    return 1.0 / (1.0 + jnp.exp(-x))


def _dot(a, b):
    return jnp.dot(a, b, preferred_element_type=F32)


def _swiglu_half_step(x, g, wg_ref, wu_ref, wd_ref):
    xn = _rms(x, g).astype(BF16)
    ff = wg_ref.shape[1]
    acts = []
    for lo in range(0, ff, MXU_COLS):
        gate = _dot(xn, wg_ref[:, lo:lo + MXU_COLS])
        up = _dot(xn, wu_ref[:, lo:lo + MXU_COLS])
        acts.append((gate * _sigmoid(gate) * up).astype(BF16))
    act = jnp.concatenate(acts, axis=1)
    return x + 0.5 * _dot(act, wd_ref[...])


def _cast_plan(weights, steps):
    in_specs, out_specs, out_shape = [], [], []
    for w in weights:
        rows, cols = w.shape
        n = max(k for k in range(1, steps + 1)
                if rows % k == 0 and (rows // k) % BF16_ROWS == 0)
        spec = pl.BlockSpec((rows // n, cols), lambda i, n=n: (jnp.minimum(i, n - 1), 0))
        in_specs.append(spec)
        out_specs.append(spec)
        out_shape.append(jax.ShapeDtypeStruct((rows, cols), BF16))
    return in_specs, out_specs, out_shape


def _cast_slabs(src_refs, dst_refs):
    for src, dst in zip(src_refs, dst_refs):
        dst[...] = src[...].astype(BF16)


def _rope_kernel(pos_lo_ref, pos_hi_ref, freq_ref, *refs):
    n_cast = (len(refs) - 2) // 2
    cos_ref, sin_ref = refs[n_cast:n_cast + 2]
    lane = lax.broadcasted_iota(jnp.int32, cos_ref.shape, 1)
    pos = jnp.where(lane < HEAD_DIM // 2, pos_lo_ref[...], pos_hi_ref[...])
    ang = pos.astype(F32) * freq_ref[...]
    cos_ref[...] = jnp.cos(ang)
    sin_ref[...] = jnp.sin(ang)
    _cast_slabs(refs[:n_cast], refs[n_cast + 2:])


def _rope_tables(positions, tm, cast_weights):
    t = positions.size
    half_t = t // 2
    inv_freq = ROPE_THETA ** (-jnp.arange(0, HEAD_DIM, 2, dtype=F32) / HEAD_DIM)
    freq = jnp.concatenate([inv_freq, inv_freq])[None, :]
    pos = positions.reshape(2, half_t, 1)
    tile = pl.BlockSpec((tm, HEAD_DIM), lambda i: (i, 0))
    col = pl.BlockSpec((tm, 1), lambda i: (i, 0))
    steps = half_t // tm
    c_in, c_out, c_shape = _cast_plan(cast_weights, steps)
    outs = pl.pallas_call(
        _rope_kernel,
        grid=(steps,),
        in_specs=[col, col, pl.BlockSpec((1, HEAD_DIM), lambda i: (0, 0))] + c_in,
        out_specs=[tile, tile] + c_out,
        out_shape=[jax.ShapeDtypeStruct((half_t, HEAD_DIM), F32)] * 2 + c_shape,
        compiler_params=_cparams(1),
        name="rope_tables",
    )(pos[0], pos[1], freq, *cast_weights)
    return outs[0], outs[1], outs[2:]


def _unpack_rope(cos_ref, sin_ref, upper_half):
    half = HEAD_DIM // 2
    lane = lax.broadcasted_iota(jnp.int32, cos_ref.shape, 1)
    low = lane < half
    own = jnp.logical_xor(low, upper_half)
    c, s = cos_ref[...], sin_ref[...]
    cos = jnp.where(own, c, pltpu.roll(c, half, axis=1))
    sin = jnp.where(own, s, pltpu.roll(s, half, axis=1))
    return cos, jnp.where(low, -sin, sin)


def _ffn1_kernel(x_ref, g1_ref, wg_ref, wu_ref, wd_ref, gmix_ref, *refs):
    n_cast = (len(refs) - 2) // 2
    h_ref, u_ref = refs[n_cast:n_cast + 2]
    n = x_ref.shape[0] // FFN1_PARTS
    for part in range(FFN1_PARTS):
        rows = slice(part * n, (part + 1) * n)
        h = _swiglu_half_step(x_ref[rows, :], g1_ref[...], wg_ref, wu_ref, wd_ref)
        h_ref[rows, :] = h
        u_ref[rows, :] = _rms(h, gmix_ref[...]).astype(BF16)
    _cast_slabs(refs[:n_cast], refs[n_cast + 2:])


def _ffn1(x, g1, wg, wu, wd, gmix, tm, cast_weights):
    t, d = x.shape
    ff = wg.shape[1]
    tile = pl.BlockSpec((tm, d), lambda i: (i, 0))
    steps = t // tm
    c_in, c_out, c_shape = _cast_plan(cast_weights, steps)
    outs = pl.pallas_call(
        _ffn1_kernel,
        grid=(steps,),
        in_specs=[tile, _resident((1, d)), _resident((d, ff)), _resident((d, ff)),
                  _resident((ff, d)), _resident((1, d))] + c_in,
        out_specs=[tile, tile] + c_out,
        out_shape=[jax.ShapeDtypeStruct((t, d), F32), jax.ShapeDtypeStruct((t, d), BF16)] + c_shape,
        compiler_params=_cparams(1),
        name="ffn1",
    )(x, g1, wg, wu, wd, gmix, *cast_weights)
    return outs[0], outs[1], outs[2:]


def _rotary(t, cos, sin_signed):
    return t * cos + pltpu.roll(t, HEAD_DIM // 2, axis=1) * sin_signed


CONV_OFF = CONV_HALO - (CONV_WIDTH - 1)
CONV_SLABS = 4
CONV_MXU_PHASES = (2, 3, 4, 5, 6, 7)


def _build_conv_diag(w_ref, dmat):
    eye = (lax.broadcasted_iota(jnp.int32, (LANES, LANES), 0)
           == lax.broadcasted_iota(jnp.int32, (LANES, LANES), 1))
    for c in range(dmat.shape[0]):
        for q in range(CONV_SLABS):
            for g, r in enumerate(CONV_MXU_PHASES):
                j = SUBLANES * q + r - CONV_OFF
                assert 0 <= j < CONV_WIDTH
                tap = w_ref[j:j + 1, c * LANES:(c + 1) * LANES]
                dmat[c, q * LANES:(q + 1) * LANES, g * LANES:(g + 1) * LANES] = (
                    jnp.where(eye, tap, 0.0).astype(BF16))


def _conv_lane_chunk(xbuf, w_ref, b_ref, dmat, acc_ref, tm, c):
    cols = slice(c * LANES, (c + 1) * LANES)
    taps = [w_ref[j:j + 1, cols] for j in range(CONV_WIDTH)]
    bias = b_ref[:, cols]
    m_rows = tm + BF16_ROWS
    slabs = [xbuf[SUBLANES * q:SUBLANES * q + m_rows, cols].astype(BF16) for q in range(CONV_SLABS)]
    phase_sums = _dot(jnp.concatenate(slabs, axis=1), dmat[c])
    for rb in range(tm // CONV_ROWS):
        base = rb * CONV_ROWS
        out = None
        for r in range(SUBLANES):
            if r in CONV_MXU_PHASES:
                g = CONV_MXU_PHASES.index(r)
                shifted = phase_sums[base + r:base + r + CONV_ROWS, g * LANES:(g + 1) * LANES]
            else:
                n = CONV_ROWS if r == 0 else CONV_ROWS + SUBLANES
                part = None
                for q in range((CONV_OFF + CONV_WIDTH - 1) // SUBLANES + 1):
                    j = SUBLANES * q + r - CONV_OFF
                    if 0 <= j < CONV_WIDTH:
                        lo = base + SUBLANES * q
                        term = taps[j] * xbuf[lo:lo + n, cols]
                        part = term if part is None else part + term
                shifted = part if r == 0 else part[r:r + CONV_ROWS]
            out = shifted if out is None else out + shifted
        acc_ref[base:base + CONV_ROWS, cols] = out + bias


def _mixer_in_kernel(u_ref, w_ref, cos_ref, sin_ref, wdw_ref, bdw_ref, gln_ref, bln_ref, wco_ref,
                     gyc_ref, q0_ref, q1_ref, q2_ref, k0_ref, k1_ref, k2_ref,
                     v0_ref, v1_ref, v2_ref, gb_ref, xbuf, acc_ref, ga_buf, rope_buf, perm, dmat,
                     *, tm, c_conv, d_model):
    dils = [dil for _, dil in GROUPS]

    @pl.when((pl.program_id(0) == 0) & (pl.program_id(1) == 0))
    def _():
        _build_conv_diag(wdw_ref, dmat)
        xbuf[CONV_HALO + tm:, :] = jnp.zeros((BF16_ROWS, c_conv), F32)
        dst = lax.broadcasted_iota(jnp.int32, (tm, tm), 0)
        tok = lax.broadcasted_iota(jnp.int32, (tm, tm), 1)
        for g, dil in enumerate(dils):
            if dil > 1:
                n = tm // dil
                perm[g] = jnp.where(tok == (dst % n) * dil + dst // n, 1.0, 0.0).astype(BF16)

    @pl.when(pl.program_id(1) == 0)
    def _():
        xbuf[0:CONV_HALO, :] = jnp.zeros((CONV_HALO, c_conv), F32)

    u = u_ref[...]

    def proj(lo, width, lhs=None):
        return _dot(u if lhs is None else lhs, w_ref[:, lo:lo + width])

    a = proj(0, c_conv)
    gate = proj(c_conv, c_conv)
    xbuf[CONV_HALO:CONV_HALO + tm, :] = a * _sigmoid(gate)

    base = 2 * c_conv
    gbase = base + 3 * ATTN_W

    tile_idx = pl.program_id(0) * pl.num_programs(1) + pl.program_id(1)
    n_tiles = pl.num_programs(0) * pl.num_programs(1)
    cos, sin = _unpack_rope(cos_ref, sin_ref, 2 * tile_idx >= n_tiles)
    rope_buf[0] = cos
    rope_buf[1] = sin

    def strided_rows(g):
        dil = dils[g]
        if dil == 1:
            return u, cos, sin
        n = tm // dil
        u_g = _dot(perm[g], u).astype(BF16)
        cos_g = jnp.concatenate([rope_buf[0, pl.ds(r, n, stride=dil), :] for r in range(dil)], axis=0)
        sin_g = jnp.concatenate([rope_buf[1, pl.ds(r, n, stride=dil), :] for r in range(dil)], axis=0)
        return u_g, cos_g, sin_g

    def emit(out_ref, val, dil):
        n = tm // dil
        for r in range(dil):
            out_ref[:, r * GROUP_W:(r + 1) * GROUP_W] = val[r * n:(r + 1) * n, :].astype(BF16)

    def group_pieces(g, q_ref, k_ref, v_ref):
        dil = dils[g]
        cache = []

        def inputs():
            if not cache:
                cache.append(strided_rows(g))
            return cache[0]

        def rotary_piece(out_ref, lo, scale):
            def piece():
                u_g, cos_g, sin_g = inputs()
                t = proj(lo, GROUP_W, u_g)
                heads = [_rotary(t[:, h * HEAD_DIM:(h + 1) * HEAD_DIM], cos_g, sin_g)
                         for h in range(HEADS_PER_GROUP)]
                if scale is not None:
                    heads = [x * scale for x in heads]
                emit(out_ref, jnp.concatenate(heads, axis=1), dil)
            return piece

        def value_piece():
            emit(v_ref, proj(base + 2 * ATTN_W + g * GROUP_W, GROUP_W, inputs()[0]), dil)

        return [rotary_piece(q_ref, base + g * GROUP_W, HEAD_DIM ** -0.5),
                rotary_piece(k_ref, base + ATTN_W + g * GROUP_W, None), value_piece]

    def gate_a_piece():
        ga_buf[...] = _sigmoid(proj(gbase, d_model))

    def gate_b_piece():
        gb_ref[...] = _sigmoid(proj(gbase + d_model, d_model)).astype(BF16)

    pieces = [gate_a_piece, gate_b_piece]
    for g, refs in enumerate(((q0_ref, k0_ref, v0_ref), (q1_ref, k1_ref, v1_ref),
                              (q2_ref, k2_ref, v2_ref))):
        pieces += group_pieces(g, *refs)

    for c in range(c_conv // LANES):
        _conv_lane_chunk(xbuf, wdw_ref, bdw_ref, dmat, acc_ref, tm, c)
    xbuf[0:CONV_HALO, :] = xbuf[tm:tm + CONV_HALO, :]
    pieces[0]()

    acc = acc_ref[...]
    mu = jnp.mean(acc, axis=-1, keepdims=True)
    cen = acc - mu
    var = jnp.mean(cen * cen, axis=-1, keepdims=True)
    y = cen * lax.rsqrt(var + EPS) * gln_ref[...] + bln_ref[...]
    y = (y * _sigmoid(y)).astype(BF16)
    gyc_ref[...] = (ga_buf[...] * _dot(y, wco_ref[...])).astype(BF16)

    for piece in pieces[1:]:
        piece()


def _mixer_in(u, w_in, cos, sin, w_dw, b_dw, g_ln, b_ln, w_co, batch, tm):
    t, d = u.shape
    in_w = w_in.shape[1]
    c_conv = w_dw.shape[1]
    per_seq = t // batch // tm
    idx = lambda b, i: (b * per_seq + i, 0)
    row = lambda w: pl.BlockSpec((tm, w), idx)
    half_tiles = t // 2 // tm
    rope = pl.BlockSpec((tm, HEAD_DIM), lambda b, i: (lax.rem(b * per_seq + i, half_tiles), 0))
    strided = lambda dil: pl.BlockSpec((tm // dil, dil * GROUP_W), idx)
    strided_sds = lambda dil: jax.ShapeDtypeStruct((t // dil, dil * GROUP_W), BF16)
    dils = [dil for _, dil in GROUPS]
    out_specs = [row(d)] + [strided(dil) for _ in range(3) for dil in dils] + [row(d)]
    out_shape = ([jax.ShapeDtypeStruct((t, d), BF16)]
                 + [strided_sds(dil) for _ in range(3) for dil in dils]
                 + [jax.ShapeDtypeStruct((t, d), BF16)])
    return pl.pallas_call(
        functools.partial(_mixer_in_kernel, tm=tm, c_conv=c_conv, d_model=d),
        grid=(batch, per_seq),
        in_specs=[row(d), _resident((d, in_w)), rope, rope,
                  _resident((CONV_WIDTH, c_conv)), _resident((1, c_conv)), _resident((1, c_conv)),
                  _resident((1, c_conv)), _resident((c_conv, d))],
        out_specs=out_specs,
        out_shape=out_shape,
        scratch_shapes=[pltpu.VMEM((CONV_HALO + tm + BF16_ROWS, c_conv), F32),
                        pltpu.VMEM((tm, c_conv), F32),
                        pltpu.VMEM((tm, d), F32),
                        pltpu.VMEM((2, tm, HEAD_DIM), F32),
                        pltpu.VMEM((N_GROUPS, tm, tm), BF16),
                        pltpu.VMEM((c_conv // LANES, CONV_SLABS * LANES,
                                    len(CONV_MXU_PHASES) * LANES), BF16)],
        compiler_params=_cparams(2),
        name="mixer_in",
    )(u, w_in, cos, sin, w_dw, b_dw, g_ln, b_ln, w_co)


def _attn_kernel(q_ref, k_ref, v_ref, kh_ref, vh_ref, o_ref, lse_ref, kbuf, vbuf,
                 *, lt, unroll, rps):
    first_tile = pl.program_id(2) == 0
    ones = jnp.ones((lt + Q_BLOCK, HEAD_DIM), BF16)
    for rr in range(rps):
        kbuf[rr, 0:Q_BLOCK, :] = kh_ref[0, :, rr * GROUP_W:(rr + 1) * GROUP_W]
        kbuf[rr, Q_BLOCK:, :] = k_ref[0, :, rr * GROUP_W:(rr + 1) * GROUP_W]
        for h in range(HEADS_PER_GROUP):
            src = slice(rr * GROUP_W + h * HEAD_DIM, rr * GROUP_W + (h + 1) * HEAD_DIM)
            vbuf[rr, 0:Q_BLOCK, 2 * h * HEAD_DIM:(2 * h + 1) * HEAD_DIM] = vh_ref[0, :, src]
            vbuf[rr, Q_BLOCK:, 2 * h * HEAD_DIM:(2 * h + 1) * HEAD_DIM] = v_ref[0, :, src]
            vbuf[rr, :, (2 * h + 1) * HEAD_DIM:(2 * h + 2) * HEAD_DIM] = ones

    qi = lax.broadcasted_iota(jnp.int32, (Q_BLOCK, 2 * Q_BLOCK), 0)
    ki = lax.broadcasted_iota(jnp.int32, (Q_BLOCK, 2 * Q_BLOCK), 1)
    dist = qi + Q_BLOCK - ki
    band = (dist >= 0) & (dist <= Q_BLOCK)
    lane = lax.broadcasted_iota(jnp.int32, (Q_BLOCK, LANES), 1)
    den_lane = lane % STAT_REP >= STAT_REP // 2
    nt = (((1,), (1,)), ((), ()))

    def blocks(units):
        work = [(rr, start, mask, h) for rr, start, mask in units for h in range(HEADS_PER_GROUP)]
        col = lambda rr, h: slice(rr * GROUP_W + h * HEAD_DIM, rr * GROUP_W + (h + 1) * HEAD_DIM)
        s = [lax.dot_general(q_ref[0, pl.ds(start, Q_BLOCK), col(rr, h)],
                             kbuf[rr, pl.ds(start, 2 * Q_BLOCK), col(0, h)], nt,
                             preferred_element_type=F32) for rr, start, _, h in work]
        s = [jnp.where(mask, x, NEG) for x, (_, _, mask, _) in zip(s, work)]
        m = [jnp.max(x, axis=-1, keepdims=True) for x in s]
        e = [jnp.exp(x - mx).astype(BF16) for x, mx in zip(s, m)]
        pv = [_dot(ex, vbuf[rr, pl.ds(start, 2 * Q_BLOCK), 2 * h * HEAD_DIM:(2 * h + 2) * HEAD_DIM])
              for ex, (rr, start, _, h) in zip(e, work)]
        for u, (rr, start, _) in enumerate(units):
            rows = pl.ds(start, Q_BLOCK)
            packed = None
            for h in range(HEADS_PER_GROUP):
                i = u * HEADS_PER_GROUP + h
                den = pv[i][:, HEAD_DIM:]
                o_ref[0, rows, col(rr, h)] = pv[i][:, :HEAD_DIM].astype(BF16)
                stat = jnp.where(den_lane, den, m[i])
                packed = stat if packed is None else jnp.where(lane >= h * STAT_REP, stat, packed)
            lse_ref[0, rows, rr * LANES:(rr + 1) * LANES] = packed

    def run(units):
        for lo in range(0, len(units), unroll):
            blocks(units[lo:lo + unroll])

    per = lt // Q_BLOCK
    upr = min(unroll, per)
    first_mask = band & jnp.logical_or(ki >= Q_BLOCK, jnp.logical_not(first_tile))
    run([(rr, j * Q_BLOCK, first_mask if j == 0 else band)
         for rr in range(rps) for j in range(upr)])

    def body(n, carry):
        start = n * (upr * Q_BLOCK)
        run([(rr, pl.multiple_of(start + j * Q_BLOCK, Q_BLOCK), band)
             for rr in range(rps) for j in range(upr)])
        return carry

    lax.fori_loop(1, per // upr, body, 0)


def _attention_group(q, k, v, batch, dilation, lt, rps):
    rows, _ = q.shape
    l = rows // batch
    lt = min(lt, l)
    per = lt // Q_BLOCK
    unroll = min(ATTN_UNROLL, per * rps)
    assert l % lt == 0 and dilation % rps == 0 and per % min(unroll, per) == 0
    view = lambda x: x.reshape(batch, l, x.shape[-1])
    tile = lambda width: pl.BlockSpec((1, lt, rps * width), lambda bi, r, i: (bi, i, r))
    halo = pl.BlockSpec((1, Q_BLOCK, rps * GROUP_W),
                        lambda bi, r, i: (bi, jnp.maximum(i * per - 1, 0), r))
    o, lse = pl.pallas_call(
        functools.partial(_attn_kernel, lt=lt, unroll=unroll, rps=rps),
        grid=(batch, dilation // rps, l // lt),
        in_specs=[tile(GROUP_W), tile(GROUP_W), tile(GROUP_W), halo, halo],
        out_specs=[tile(GROUP_W), tile(LANES)],
        out_shape=[jax.ShapeDtypeStruct((batch, l, dilation * GROUP_W), BF16),
                   jax.ShapeDtypeStruct((batch, l, dilation * LANES), F32)],
        scratch_shapes=[pltpu.VMEM((rps, lt + Q_BLOCK, GROUP_W), BF16),
                        pltpu.VMEM((rps, lt + Q_BLOCK, 2 * GROUP_W), BF16)],
        compiler_params=_cparams(3),
        name=f"attn_d{dilation}",
    )(view(q), view(k), view(v), view(k), view(v))
    return o.reshape(rows, dilation * GROUP_W), lse.reshape(rows, dilation * LANES)


def _tail_kernel(h_ref, gyc_ref, gb_ref, o0_ref, o1_ref, o2_ref, l0_ref, l1_ref, l2_ref, p_ref,
                 wao_ref, wmix_ref, g2_ref, wg_ref, wu_ref, wd_ref, gple_ref, wpg_ref, wpp_ref,
                 gfin_ref, y_ref, obuf, lbuf, spread, *, tm, final_norm):
    @pl.when(pl.program_id(0) == 0)
    def _():
        src = lax.broadcasted_iota(jnp.int32, spread.shape, 0) % LANES
        dst_head = lax.broadcasted_iota(jnp.int32, spread.shape, 1) // HEAD_DIM
        spread[...] = jnp.where(src == dst_head * STAT_REP, 1.0, 0.0).astype(BF16)

    parts = [_mixer_out_rows(part, tm // TAIL_PARTS, h_ref, gyc_ref, gb_ref,
                             (o0_ref, o1_ref, o2_ref), (l0_ref, l1_ref, l2_ref),
                             wao_ref, wmix_ref, obuf, lbuf, spread) for part in range(TAIL_PARTS)]
    h = parts[0] if TAIL_PARTS == 1 else jnp.concatenate(parts, axis=0)
    h = _swiglu_half_step(h, g2_ref[...], wg_ref, wu_ref, wd_ref)
    gate = _sigmoid(_dot(_rms(h, gple_ref[...]).astype(BF16), wpg_ref[...]))
    h = h + gate * _dot(p_ref[...].astype(BF16), wpp_ref[...])
    y_ref[...] = _rms(h, gfin_ref[...]) if final_norm else h


def _mixer_out_rows(part, tp, h_ref, gyc_ref, gb_ref, o_refs, l_refs, wao_ref, wmix_ref,
                    obuf, lbuf, spread):
    rows_tok = slice(part * tp, (part + 1) * tp)

    def token_order(dst, src_ref, dil, width):
        chunks = width // LANES
        n = tp // dil
        src_rows = slice(part * n, (part + 1) * n)
        if dil == 1:
            return [src_ref[src_rows, c * LANES:(c + 1) * LANES].astype(F32) for c in range(chunks)]
        for r in range(dil):
            for c in range(chunks):
                lo = r * width + c * LANES
                dst[c, pl.ds(part * tp + r, n, stride=dil), :] = (
                    src_ref[src_rows, lo:lo + LANES].astype(F32))
        return [dst[c, rows_tok, :] for c in range(chunks)]

    dils = [dil for _, dil in GROUPS]
    ls = [token_order(lbuf.at[g], ref, dils[g], LANES)[0] for g, ref in enumerate(l_refs)]
    os_ = [token_order(obuf.at[g], ref, dils[g], GROUP_W) for g, ref in enumerate(o_refs)]

    half = STAT_REP // 2
    lane = lax.broadcasted_iota(jnp.int32, ls[0].shape, 1)
    m = jnp.maximum(jnp.maximum(ls[0], ls[1]), ls[2])
    es = [jnp.exp(x - m) for x in ls]
    dens = [pltpu.roll(x, LANES - half, axis=1) for x in ls]
    inv = 1.0 / (es[0] * dens[0] + es[1] * dens[1] + es[2] * dens[2])
    alphas = [jnp.where(lane % STAT_REP < half, x * inv, 0.0) for x in es]
    wide = []
    for a in alphas:
        hi = a.astype(BF16)
        lo = (a - hi.astype(F32)).astype(BF16)
        wide.append(_dot(jnp.concatenate([hi, lo], axis=1), spread[...]))
    heads = []
    for h in range(HEADS_PER_GROUP):
        acc = None
        for g in range(N_GROUPS):
            term = wide[g][:, h * HEAD_DIM:(h + 1) * HEAD_DIM] * os_[g][h]
            acc = term if acc is None else acc + term
        heads.append(acc.astype(BF16))
    o = jnp.concatenate(heads, axis=1)
    y_attn = _dot(o, wao_ref[...])
    mixed = gyc_ref[rows_tok, :].astype(F32) + gb_ref[rows_tok, :].astype(F32) * y_attn
    return h_ref[rows_tok, :] + _dot(mixed.astype(BF16), wmix_ref[...])


def _tail(h, gyc, gb, os_, ls_, p, wao, wmix, g2, wg, wu, wd, gple, wpg, wpp, gfin, tm,
          final_norm):
    t, d = h.shape
    row = lambda w: pl.BlockSpec((tm, w), lambda i: (i, 0))
    strided = lambda dil, w: pl.BlockSpec((tm // dil, dil * w), lambda i: (i, 0))
    res = lambda x: _resident(x.shape)
    weights = (wao, wmix, g2, wg, wu, wd, gple, wpg, wpp, gfin)
    dils = [dil for _, dil in GROUPS]
    return pl.pallas_call(
        functools.partial(_tail_kernel, tm=tm, final_norm=final_norm),
        grid=(t // tm,),
        in_specs=[row(d), row(d), row(d)] + [strided(dil, GROUP_W) for dil in dils]
                 + [strided(dil, LANES) for dil in dils]
                 + [row(p.shape[1])] + [res(x) for x in weights],
        out_specs=row(d),
        out_shape=jax.ShapeDtypeStruct((t, d), F32),
        scratch_shapes=[pltpu.VMEM((N_GROUPS, GROUP_W // LANES, tm, LANES), F32),
                        pltpu.VMEM((N_GROUPS, 1, tm, LANES), F32),
                        pltpu.VMEM((2 * LANES, GROUP_W), BF16)],
        compiler_params=_cparams(1),
        name="tail",
    )(h, gyc, gb, *os_, *ls_, p, *weights)


def kernel(x, p, positions, g_ffn1, w_ffn1_gate, w_ffn1_up, w_ffn1_down, g_mix, w_in,
           w_dw, b_dw, g_conv_ln, b_conv_ln, w_conv_out, w_attn_out, w_mix_out,
           g_ffn2, w_ffn2_gate, w_ffn2_up, w_ffn2_down, g_ple, w_ple_gate, w_ple_proj,
           g_final):
    b, s, d = x.shape
    depth = p.shape[0]
    t = b * s
    c_conv = w_dw.shape[-1]
    assert w_in.shape[-1] == 2 * c_conv + 3 * ATTN_W + 2 * d
    tm, tm_mix = 512, 256
    max_dil = max(dil for _, dil in GROUPS)
    assert s % tm == 0 and s % tm_mix == 0
    assert tm_mix % CONV_ROWS == 0 and (tm_mix // max_dil) % 16 == 0
    for window, dil in GROUPS:
        assert window // dil == Q_BLOCK and (s // dil) % Q_BLOCK == 0

    assert depth == 1
    i = 0
    row = lambda v: v.reshape(1, -1)
    cos, sin, (wg1, wu1, wd1) = _rope_tables(
        positions, ROPE_ROWS, [w_ffn1_gate[i], w_ffn1_up[i], w_ffn1_down[i]])
    h, u, (win, wco, wao, wmix, wg2, wu2, wd2, wpg, wpp) = _ffn1(
        x.reshape(t, d), row(g_ffn1[i]), wg1, wu1, wd1, row(g_mix[i]), FFN1_ROWS,
        [w_in[i], w_conv_out[i], w_attn_out[i], w_mix_out[i], w_ffn2_gate[i], w_ffn2_up[i],
         w_ffn2_down[i], w_ple_gate[i], w_ple_proj[i]])
    gyc, q0, q1, q2, k0, k1, k2, v0, v1, v2, gb = _mixer_in(
        u, win, cos, sin, w_dw[i], row(b_dw[i]), row(g_conv_ln[i]), row(b_conv_ln[i]), wco,
        b, tm_mix)
    os_, ls_ = [], []
    for (_, dil), q, k, v in zip(GROUPS, (q0, q1, q2), (k0, k1, k2), (v0, v1, v2)):
        o, lse = _attention_group(q, k, v, b, dil, ATTN_ROWS, max(1, ATTN_ROWS * dil // s))
        os_.append(o)
        ls_.append(lse)
    h = _tail(h, gyc, gb, os_, ls_, p[i].reshape(t, -1), wao, wmix, row(g_ffn2[i]), wg2, wu2,
              wd2, row(g_ple[i]), wpg, wpp, row(g_final), tm, True)
    return h.reshape(b, s, d)
```

```python
import functools

import jax
import jax.numpy as jnp
from jax import lax
from jax.experimental import pallas as pl
from jax.experimental.pallas import tpu as pltpu

HEAD_DIM = 128
HEADS_PER_GROUP = 4
GROUPS = ((128, 1), (512, 4), (2048, 16))
N_GROUPS = len(GROUPS)
GROUP_W = HEADS_PER_GROUP * HEAD_DIM
ATTN_W = N_GROUPS * GROUP_W
CONV_WIDTH = 31
Q_BLOCK = 128
ROPE_THETA = 10000.0
EPS = 1e-6
NEG = -1e30

LANES = 128
SUBLANES = 8
BF16_ROWS = 16
MXU_COLS = 256
CONV_HALO = 32
CONV_ROWS = 128
ATTN_UNROLL = 4
ATTN_ROWS = 2048
ROPE_ROWS = 1024
FFN1_PARTS = 2
TAIL_PARTS = 1
STAT_REP = LANES // HEADS_PER_GROUP
VMEM_LIMIT = 56 * 1024 * 1024

BF16 = jnp.bfloat16
F32 = jnp.float32


def _cparams(n_axes):
    return pltpu.CompilerParams(
        dimension_semantics=("arbitrary",) * n_axes, vmem_limit_bytes=VMEM_LIMIT)


def _resident(shape):
    nd = len(shape)
    return pl.BlockSpec(shape, lambda *_: (0,) * nd, pipeline_mode=pl.Buffered(1))


def _rms(x, g):
    ms = jnp.mean(x * x, axis=-1, keepdims=True)
    return x * lax.rsqrt(ms + EPS) * g


def _sigmoid(x):
    return 1.0 / (1.0 + jnp.exp(-x))


def _dot(a, b):
    return jnp.dot(a, b, preferred_element_type=F32)


def _swiglu_half_step(x, g, wg_ref, wu_ref, wd_ref):
    xn = _rms(x, g).astype(BF16)
    ff = wg_ref.shape[1]
    acts = []
    for lo in range(0, ff, MXU_COLS):
        gate = _dot(xn, wg_ref[:, lo:lo + MXU_COLS])
        up = _dot(xn, wu_ref[:, lo:lo + MXU_COLS])
        acts.append((gate * _sigmoid(gate) * up).astype(BF16))
    act = jnp.concatenate(acts, axis=1)
    return x + 0.5 * _dot(act, wd_ref[...])


def _cast_plan(weights, steps):
    in_specs, out_specs, out_shape = [], [], []
    for w in weights:
        rows, cols = w.shape
        n = max(k for k in range(1, steps + 1)
                if rows % k == 0 and (rows // k) % BF16_ROWS == 0)
        spec = pl.BlockSpec((rows // n, cols), lambda i, n=n: (jnp.minimum(i, n - 1), 0))
        in_specs.append(spec)
        out_specs.append(spec)
        out_shape.append(jax.ShapeDtypeStruct((rows, cols), BF16))
    return in_specs, out_specs, out_shape


def _cast_slabs(src_refs, dst_refs):
    for src, dst in zip(src_refs, dst_refs):
        dst[...] = src[...].astype(BF16)


def _rope_kernel(pos_lo_ref, pos_hi_ref, freq_ref, *refs):
    n_cast = (len(refs) - 2) // 2
    cos_ref, sin_ref = refs[n_cast:n_cast + 2]
    lane = lax.broadcasted_iota(jnp.int32, cos_ref.shape, 1)
    pos = jnp.where(lane < HEAD_DIM // 2, pos_lo_ref[...], pos_hi_ref[...])
    ang = pos.astype(F32) * freq_ref[...]
    cos_ref[...] = jnp.cos(ang)
    sin_ref[...] = jnp.sin(ang)
    _cast_slabs(refs[:n_cast], refs[n_cast + 2:])


def _rope_tables(positions, tm, cast_weights):
    t = positions.size
    half_t = t // 2
    inv_freq = ROPE_THETA ** (-jnp.arange(0, HEAD_DIM, 2, dtype=F32) / HEAD_DIM)
    freq = jnp.concatenate([inv_freq, inv_freq])[None, :]
    pos = positions.reshape(2, half_t, 1)
    tile = pl.BlockSpec((tm, HEAD_DIM), lambda i: (i, 0))
    col = pl.BlockSpec((tm, 1), lambda i: (i, 0))
    steps = half_t // tm
    c_in, c_out, c_shape = _cast_plan(cast_weights, steps)
    outs = pl.pallas_call(
        _rope_kernel,
        grid=(steps,),
        in_specs=[col, col, pl.BlockSpec((1, HEAD_DIM), lambda i: (0, 0))] + c_in,
        out_specs=[tile, tile] + c_out,
        out_shape=[jax.ShapeDtypeStruct((half_t, HEAD_DIM), F32)] * 2 + c_shape,
        compiler_params=_cparams(1),
        name="rope_tables",
    )(pos[0], pos[1], freq, *cast_weights)
    return outs[0], outs[1], outs[2:]


def _unpack_rope(cos_ref, sin_ref, upper_half):
    half = HEAD_DIM // 2
    lane = lax.broadcasted_iota(jnp.int32, cos_ref.shape, 1)
    low = lane < half
    own = jnp.logical_xor(low, upper_half)
    c, s = cos_ref[...], sin_ref[...]
    cos = jnp.where(own, c, pltpu.roll(c, half, axis=1))
    sin = jnp.where(own, s, pltpu.roll(s, half, axis=1))
    return cos, jnp.where(low, -sin, sin)


def _ffn1_kernel(x_ref, g1_ref, wg_ref, wu_ref, wd_ref, gmix_ref, *refs):
    n_cast = (len(refs) - 2) // 2
    h_ref, u_ref = refs[n_cast:n_cast + 2]
    n = x_ref.shape[0] // FFN1_PARTS
    for part in range(FFN1_PARTS):
        rows = slice(part * n, (part + 1) * n)
        h = _swiglu_half_step(x_ref[rows, :], g1_ref[...], wg_ref, wu_ref, wd_ref)
        h_ref[rows, :] = h
        u_ref[rows, :] = _rms(h, gmix_ref[...]).astype(BF16)
    _cast_slabs(refs[:n_cast], refs[n_cast + 2:])


def _ffn1(x, g1, wg, wu, wd, gmix, tm, cast_weights):
    t, d = x.shape
    ff = wg.shape[1]
    tile = pl.BlockSpec((tm, d), lambda i: (i, 0))
    steps = t // tm
    c_in, c_out, c_shape = _cast_plan(cast_weights, steps)
    outs = pl.pallas_call(
        _ffn1_kernel,
        grid=(steps,),
        in_specs=[tile, _resident((1, d)), _resident((d, ff)), _resident((d, ff)),
                  _resident((ff, d)), _resident((1, d))] + c_in,
        out_specs=[tile, tile] + c_out,
        out_shape=[jax.ShapeDtypeStruct((t, d), F32), jax.ShapeDtypeStruct((t, d), BF16)] + c_shape,
        compiler_params=_cparams(1),
        name="ffn1",
    )(x, g1, wg, wu, wd, gmix, *cast_weights)
    return outs[0], outs[1], outs[2:]


def _rotary(t, cos, sin_signed):
    return t * cos + pltpu.roll(t, HEAD_DIM // 2, axis=1) * sin_signed


CONV_OFF = CONV_HALO - (CONV_WIDTH - 1)
CONV_SLABS = 4
CONV_MXU_PHASES = (4, 5, 6, 7)


def _build_conv_diag(w_ref, dmat):
    eye = (lax.broadcasted_iota(jnp.int32, (LANES, LANES), 0)
           == lax.broadcasted_iota(jnp.int32, (LANES, LANES), 1))
    for c in range(dmat.shape[0]):
        for q in range(CONV_SLABS):
            for g, r in enumerate(CONV_MXU_PHASES):
                j = SUBLANES * q + r - CONV_OFF
                assert 0 <= j < CONV_WIDTH
                tap = w_ref[j:j + 1, c * LANES:(c + 1) * LANES]
                dmat[c, q * LANES:(q + 1) * LANES, g * LANES:(g + 1) * LANES] = (
                    jnp.where(eye, tap, 0.0).astype(BF16))


def _conv_lane_chunk(xbuf, w_ref, b_ref, dmat, acc_ref, tm, c):
    cols = slice(c * LANES, (c + 1) * LANES)
    taps = [w_ref[j:j + 1, cols] for j in range(CONV_WIDTH)]
    bias = b_ref[:, cols]
    m_rows = tm + BF16_ROWS
    slabs = [xbuf[SUBLANES * q:SUBLANES * q + m_rows, cols].astype(BF16) for q in range(CONV_SLABS)]
    phase_sums = _dot(jnp.concatenate(slabs, axis=1), dmat[c])
    for rb in range(tm // CONV_ROWS):
        base = rb * CONV_ROWS
        out = None
        for r in range(SUBLANES):
            if r in CONV_MXU_PHASES:
                g = CONV_MXU_PHASES.index(r)
                shifted = phase_sums[base + r:base + r + CONV_ROWS, g * LANES:(g + 1) * LANES]
            else:
                n = CONV_ROWS if r == 0 else CONV_ROWS + SUBLANES
                part = None
                for q in range((CONV_OFF + CONV_WIDTH - 1) // SUBLANES + 1):
                    j = SUBLANES * q + r - CONV_OFF
                    if 0 <= j < CONV_WIDTH:
                        lo = base + SUBLANES * q
                        term = taps[j] * xbuf[lo:lo + n, cols]
                        part = term if part is None else part + term
                shifted = part if r == 0 else part[r:r + CONV_ROWS]
            out = shifted if out is None else out + shifted
        acc_ref[base:base + CONV_ROWS, cols] = out + bias


def _mixer_in_kernel(u_ref, w_ref, cos_ref, sin_ref, wdw_ref, bdw_ref, gln_ref, bln_ref, wco_ref,
                     gyc_ref, q0_ref, q1_ref, q2_ref, k0_ref, k1_ref, k2_ref,
                     v0_ref, v1_ref, v2_ref, gb_ref, xbuf, acc_ref, ga_buf, rope_buf, perm, dmat,
                     *, tm, c_conv, d_model):
    dils = [dil for _, dil in GROUPS]

    @pl.when((pl.program_id(0) == 0) & (pl.program_id(1) == 0))
    def _():
        _build_conv_diag(wdw_ref, dmat)
        xbuf[CONV_HALO + tm:, :] = jnp.zeros((BF16_ROWS, c_conv), F32)
        dst = lax.broadcasted_iota(jnp.int32, (tm, tm), 0)
        tok = lax.broadcasted_iota(jnp.int32, (tm, tm), 1)
        for g, dil in enumerate(dils):
            if dil > 1:
                n = tm // dil
                perm[g] = jnp.where(tok == (dst % n) * dil + dst // n, 1.0, 0.0).astype(BF16)

    @pl.when(pl.program_id(1) == 0)
    def _():
        xbuf[0:CONV_HALO, :] = jnp.zeros((CONV_HALO, c_conv), F32)

    u = u_ref[...]

    def proj(lo, width, lhs=None):
        return _dot(u if lhs is None else lhs, w_ref[:, lo:lo + width])

    a = proj(0, c_conv)
    gate = proj(c_conv, c_conv)
    xbuf[CONV_HALO:CONV_HALO + tm, :] = a * _sigmoid(gate)

    base = 2 * c_conv
    gbase = base + 3 * ATTN_W

    tile_idx = pl.program_id(0) * pl.num_programs(1) + pl.program_id(1)
    n_tiles = pl.num_programs(0) * pl.num_programs(1)
    cos, sin = _unpack_rope(cos_ref, sin_ref, 2 * tile_idx >= n_tiles)
    rope_buf[0] = cos
    rope_buf[1] = sin

    def strided_rows(g):
        dil = dils[g]
        if dil == 1:
            return u, cos, sin
        n = tm // dil
        u_g = _dot(perm[g], u).astype(BF16)
        cos_g = jnp.concatenate([rope_buf[0, pl.ds(r, n, stride=dil), :] for r in range(dil)], axis=0)
        sin_g = jnp.concatenate([rope_buf[1, pl.ds(r, n, stride=dil), :] for r in range(dil)], axis=0)
        return u_g, cos_g, sin_g

    def emit(out_ref, val, dil):
        n = tm // dil
        for r in range(dil):
            out_ref[:, r * GROUP_W:(r + 1) * GROUP_W] = val[r * n:(r + 1) * n, :].astype(BF16)

    def group_pieces(g, q_ref, k_ref, v_ref):
        dil = dils[g]
        cache = []

        def inputs():
            if not cache:
                cache.append(strided_rows(g))
            return cache[0]

        def rotary_piece(out_ref, lo, scale):
            def piece():
                u_g, cos_g, sin_g = inputs()
                t = proj(lo, GROUP_W, u_g)
                heads = [_rotary(t[:, h * HEAD_DIM:(h + 1) * HEAD_DIM], cos_g, sin_g)
                         for h in range(HEADS_PER_GROUP)]
                if scale is not None:
                    heads = [x * scale for x in heads]
                emit(out_ref, jnp.concatenate(heads, axis=1), dil)
            return piece

        def value_piece():
            emit(v_ref, proj(base + 2 * ATTN_W + g * GROUP_W, GROUP_W, inputs()[0]), dil)

        return [rotary_piece(q_ref, base + g * GROUP_W, HEAD_DIM ** -0.5),
                rotary_piece(k_ref, base + ATTN_W + g * GROUP_W, None), value_piece]

    def gate_a_piece():
        ga_buf[...] = _sigmoid(proj(gbase, d_model))

    def gate_b_piece():
        gb_ref[...] = _sigmoid(proj(gbase + d_model, d_model)).astype(BF16)

    pieces = [gate_a_piece, gate_b_piece]
    for g, refs in enumerate(((q0_ref, k0_ref, v0_ref), (q1_ref, k1_ref, v1_ref),
                              (q2_ref, k2_ref, v2_ref))):
        pieces += group_pieces(g, *refs)

    for c in range(c_conv // LANES):
        _conv_lane_chunk(xbuf, wdw_ref, bdw_ref, dmat, acc_ref, tm, c)
    xbuf[0:CONV_HALO, :] = xbuf[tm:tm + CONV_HALO, :]
    pieces[0]()

    acc = acc_ref[...]
    mu = jnp.mean(acc, axis=-1, keepdims=True)
    cen = acc - mu
    var = jnp.mean(cen * cen, axis=-1, keepdims=True)
    y = cen * lax.rsqrt(var + EPS) * gln_ref[...] + bln_ref[...]
    y = (y * _sigmoid(y)).astype(BF16)
    gyc_ref[...] = (ga_buf[...] * _dot(y, wco_ref[...])).astype(BF16)

    for piece in pieces[1:]:
        piece()


def _mixer_in(u, w_in, cos, sin, w_dw, b_dw, g_ln, b_ln, w_co, batch, tm):
    t, d = u.shape
    in_w = w_in.shape[1]
    c_conv = w_dw.shape[1]
    per_seq = t // batch // tm
    idx = lambda b, i: (b * per_seq + i, 0)
    row = lambda w: pl.BlockSpec((tm, w), idx)
    half_tiles = t // 2 // tm
    rope = pl.BlockSpec((tm, HEAD_DIM), lambda b, i: (lax.rem(b * per_seq + i, half_tiles), 0))
    strided = lambda dil: pl.BlockSpec((tm // dil, dil * GROUP_W), idx)
    strided_sds = lambda dil: jax.ShapeDtypeStruct((t // dil, dil * GROUP_W), BF16)
    dils = [dil for _, dil in GROUPS]
    out_specs = [row(d)] + [strided(dil) for _ in range(3) for dil in dils] + [row(d)]
    out_shape = ([jax.ShapeDtypeStruct((t, d), BF16)]
                 + [strided_sds(dil) for _ in range(3) for dil in dils]
                 + [jax.ShapeDtypeStruct((t, d), BF16)])
    return pl.pallas_call(
        functools.partial(_mixer_in_kernel, tm=tm, c_conv=c_conv, d_model=d),
        grid=(batch, per_seq),
        in_specs=[row(d), _resident((d, in_w)), rope, rope,
                  _resident((CONV_WIDTH, c_conv)), _resident((1, c_conv)), _resident((1, c_conv)),
                  _resident((1, c_conv)), _resident((c_conv, d))],
        out_specs=out_specs,
        out_shape=out_shape,
        scratch_shapes=[pltpu.VMEM((CONV_HALO + tm + BF16_ROWS, c_conv), F32),
                        pltpu.VMEM((tm, c_conv), F32),
                        pltpu.VMEM((tm, d), F32),
                        pltpu.VMEM((2, tm, HEAD_DIM), F32),
                        pltpu.VMEM((N_GROUPS, tm, tm), BF16),
                        pltpu.VMEM((c_conv // LANES, CONV_SLABS * LANES,
                                    len(CONV_MXU_PHASES) * LANES), BF16)],
        compiler_params=_cparams(2),
        name="mixer_in",
    )(u, w_in, cos, sin, w_dw, b_dw, g_ln, b_ln, w_co)


def _attn_kernel(q_ref, k_ref, v_ref, kh_ref, vh_ref, o_ref, lse_ref, kbuf, vbuf,
                 *, lt, unroll, rps):
    first_tile = pl.program_id(2) == 0
    ones = jnp.ones((lt + Q_BLOCK, HEAD_DIM), BF16)
    for rr in range(rps):
        kbuf[rr, 0:Q_BLOCK, :] = kh_ref[0, :, rr * GROUP_W:(rr + 1) * GROUP_W]
        kbuf[rr, Q_BLOCK:, :] = k_ref[0, :, rr * GROUP_W:(rr + 1) * GROUP_W]
        for h in range(HEADS_PER_GROUP):
            src = slice(rr * GROUP_W + h * HEAD_DIM, rr * GROUP_W + (h + 1) * HEAD_DIM)
            vbuf[rr, 0:Q_BLOCK, 2 * h * HEAD_DIM:(2 * h + 1) * HEAD_DIM] = vh_ref[0, :, src]
            vbuf[rr, Q_BLOCK:, 2 * h * HEAD_DIM:(2 * h + 1) * HEAD_DIM] = v_ref[0, :, src]
            vbuf[rr, :, (2 * h + 1) * HEAD_DIM:(2 * h + 2) * HEAD_DIM] = ones

    qi = lax.broadcasted_iota(jnp.int32, (Q_BLOCK, 2 * Q_BLOCK), 0)
    ki = lax.broadcasted_iota(jnp.int32, (Q_BLOCK, 2 * Q_BLOCK), 1)
    dist = qi + Q_BLOCK - ki
    band = (dist >= 0) & (dist <= Q_BLOCK)
    lane = lax.broadcasted_iota(jnp.int32, (Q_BLOCK, LANES), 1)
    den_lane = lane % STAT_REP >= STAT_REP // 2
    nt = (((1,), (1,)), ((), ()))

    def blocks(units):
        work = [(rr, start, mask, h) for rr, start, mask in units for h in range(HEADS_PER_GROUP)]
        col = lambda rr, h: slice(rr * GROUP_W + h * HEAD_DIM, rr * GROUP_W + (h + 1) * HEAD_DIM)
        s = [lax.dot_general(q_ref[0, pl.ds(start, Q_BLOCK), col(rr, h)],
                             kbuf[rr, pl.ds(start, 2 * Q_BLOCK), col(0, h)], nt,
                             preferred_element_type=F32) for rr, start, _, h in work]
        s = [jnp.where(mask, x, NEG) for x, (_, _, mask, _) in zip(s, work)]
        m = [jnp.max(x, axis=-1, keepdims=True) for x in s]
        e = [jnp.exp(x - mx).astype(BF16) for x, mx in zip(s, m)]
        pv = [_dot(ex, vbuf[rr, pl.ds(start, 2 * Q_BLOCK), 2 * h * HEAD_DIM:(2 * h + 2) * HEAD_DIM])
              for ex, (rr, start, _, h) in zip(e, work)]
        for u, (rr, start, _) in enumerate(units):
            rows = pl.ds(start, Q_BLOCK)
            packed = None
            for h in range(HEADS_PER_GROUP):
                i = u * HEADS_PER_GROUP + h
                den = pv[i][:, HEAD_DIM:]
                o_ref[0, rows, col(rr, h)] = pv[i][:, :HEAD_DIM].astype(BF16)
                stat = jnp.where(den_lane, den, m[i])
                packed = stat if packed is None else jnp.where(lane >= h * STAT_REP, stat, packed)
            lse_ref[0, rows, rr * LANES:(rr + 1) * LANES] = packed

    def run(units):
        for lo in range(0, len(units), unroll):
            blocks(units[lo:lo + unroll])

    per = lt // Q_BLOCK
    upr = min(unroll, per)
    first_mask = band & jnp.logical_or(ki >= Q_BLOCK, jnp.logical_not(first_tile))
    run([(rr, j * Q_BLOCK, first_mask if j == 0 else band)
         for rr in range(rps) for j in range(upr)])

    def body(n, carry):
        start = n * (upr * Q_BLOCK)
        run([(rr, pl.multiple_of(start + j * Q_BLOCK, Q_BLOCK), band)
             for rr in range(rps) for j in range(upr)])
        return carry

    lax.fori_loop(1, per // upr, body, 0)


def _attention_group(q, k, v, batch, dilation, lt, rps):
    rows, _ = q.shape
    l = rows // batch
    lt = min(lt, l)
    per = lt // Q_BLOCK
    unroll = min(ATTN_UNROLL, per * rps)
    assert l % lt == 0 and dilation % rps == 0 and per % min(unroll, per) == 0
    view = lambda x: x.reshape(batch, l, x.shape[-1])
    tile = lambda width: pl.BlockSpec((1, lt, rps * width), lambda bi, r, i: (bi, i, r))
    halo = pl.BlockSpec((1, Q_BLOCK, rps * GROUP_W),
                        lambda bi, r, i: (bi, jnp.maximum(i * per - 1, 0), r))
    o, lse = pl.pallas_call(
        functools.partial(_attn_kernel, lt=lt, unroll=unroll, rps=rps),
        grid=(batch, dilation // rps, l // lt),
        in_specs=[tile(GROUP_W), tile(GROUP_W), tile(GROUP_W), halo, halo],
        out_specs=[tile(GROUP_W), tile(LANES)],
        out_shape=[jax.ShapeDtypeStruct((batch, l, dilation * GROUP_W), BF16),
                   jax.ShapeDtypeStruct((batch, l, dilation * LANES), F32)],
        scratch_shapes=[pltpu.VMEM((rps, lt + Q_BLOCK, GROUP_W), BF16),
                        pltpu.VMEM((rps, lt + Q_BLOCK, 2 * GROUP_W), BF16)],
        compiler_params=_cparams(3),
        name=f"attn_d{dilation}",
    )(view(q), view(k), view(v), view(k), view(v))
    return o.reshape(rows, dilation * GROUP_W), lse.reshape(rows, dilation * LANES)


def _tail_kernel(h_ref, gyc_ref, gb_ref, o0_ref, o1_ref, o2_ref, l0_ref, l1_ref, l2_ref, p_ref,
                 wao_ref, wmix_ref, g2_ref, wg_ref, wu_ref, wd_ref, gple_ref, wpg_ref, wpp_ref,
                 gfin_ref, y_ref, obuf, lbuf, spread, *, tm, final_norm):
    @pl.when(pl.program_id(0) == 0)
    def _():
        src = lax.broadcasted_iota(jnp.int32, spread.shape, 0) % LANES
        dst_head = lax.broadcasted_iota(jnp.int32, spread.shape, 1) // HEAD_DIM
        spread[...] = jnp.where(src == dst_head * STAT_REP, 1.0, 0.0).astype(BF16)

    parts = [_mixer_out_rows(part, tm // TAIL_PARTS, h_ref, gyc_ref, gb_ref,
                             (o0_ref, o1_ref, o2_ref), (l0_ref, l1_ref, l2_ref),
                             wao_ref, wmix_ref, obuf, lbuf, spread) for part in range(TAIL_PARTS)]
    h = parts[0] if TAIL_PARTS == 1 else jnp.concatenate(parts, axis=0)
    h = _swiglu_half_step(h, g2_ref[...], wg_ref, wu_ref, wd_ref)
    gate = _sigmoid(_dot(_rms(h, gple_ref[...]).astype(BF16), wpg_ref[...]))
    h = h + gate * _dot(p_ref[...].astype(BF16), wpp_ref[...])
    y_ref[...] = _rms(h, gfin_ref[...]) if final_norm else h


def _mixer_out_rows(part, tp, h_ref, gyc_ref, gb_ref, o_refs, l_refs, wao_ref, wmix_ref,
                    obuf, lbuf, spread):
    rows_tok = slice(part * tp, (part + 1) * tp)

    def token_order(dst, src_ref, dil, width):
        chunks = width // LANES
        n = tp // dil
        src_rows = slice(part * n, (part + 1) * n)
        if dil == 1:
            return [src_ref[src_rows, c * LANES:(c + 1) * LANES].astype(F32) for c in range(chunks)]
        for r in range(dil):
            for c in range(chunks):
                lo = r * width + c * LANES
                dst[c, pl.ds(part * tp + r, n, stride=dil), :] = (
                    src_ref[src_rows, lo:lo + LANES].astype(F32))
        return [dst[c, rows_tok, :] for c in range(chunks)]

    dils = [dil for _, dil in GROUPS]
    ls = [token_order(lbuf.at[g], ref, dils[g], LANES)[0] for g, ref in enumerate(l_refs)]
    os_ = [token_order(obuf.at[g], ref, dils[g], GROUP_W) for g, ref in enumerate(o_refs)]

    half = STAT_REP // 2
    lane = lax.broadcasted_iota(jnp.int32, ls[0].shape, 1)
    m = jnp.maximum(jnp.maximum(ls[0], ls[1]), ls[2])
    es = [jnp.exp(x - m) for x in ls]
    dens = [pltpu.roll(x, LANES - half, axis=1) for x in ls]
    inv = 1.0 / (es[0] * dens[0] + es[1] * dens[1] + es[2] * dens[2])
    alphas = [jnp.where(lane % STAT_REP < half, x * inv, 0.0) for x in es]
    wide = []
    for a in alphas:
        hi = a.astype(BF16)
        lo = (a - hi.astype(F32)).astype(BF16)
        wide.append(_dot(jnp.concatenate([hi, lo], axis=1), spread[...]))
    heads = []
    for h in range(HEADS_PER_GROUP):
        acc = None
        for g in range(N_GROUPS):
            term = wide[g][:, h * HEAD_DIM:(h + 1) * HEAD_DIM] * os_[g][h]
            acc = term if acc is None else acc + term
        heads.append(acc.astype(BF16))
    o = jnp.concatenate(heads, axis=1)
    y_attn = _dot(o, wao_ref[...])
    mixed = gyc_ref[rows_tok, :].astype(F32) + gb_ref[rows_tok, :].astype(F32) * y_attn
    return h_ref[rows_tok, :] + _dot(mixed.astype(BF16), wmix_ref[...])


def _tail(h, gyc, gb, os_, ls_, p, wao, wmix, g2, wg, wu, wd, gple, wpg, wpp, gfin, tm,
          final_norm):
    t, d = h.shape
    row = lambda w: pl.BlockSpec((tm, w), lambda i: (i, 0))
    strided = lambda dil, w: pl.BlockSpec((tm // dil, dil * w), lambda i: (i, 0))
    res = lambda x: _resident(x.shape)
    weights = (wao, wmix, g2, wg, wu, wd, gple, wpg, wpp, gfin)
    dils = [dil for _, dil in GROUPS]
    return pl.pallas_call(
        functools.partial(_tail_kernel, tm=tm, final_norm=final_norm),
        grid=(t // tm,),
        in_specs=[row(d), row(d), row(d)] + [strided(dil, GROUP_W) for dil in dils]
                 + [strided(dil, LANES) for dil in dils]
                 + [row(p.shape[1])] + [res(x) for x in weights],
        out_specs=row(d),
        out_shape=jax.ShapeDtypeStruct((t, d), F32),
        scratch_shapes=[pltpu.VMEM((N_GROUPS, GROUP_W // LANES, tm, LANES), F32),
                        pltpu.VMEM((N_GROUPS, 1, tm, LANES), F32),
                        pltpu.VMEM((2 * LANES, GROUP_W), BF16)],
        compiler_params=_cparams(1),
        name="tail",
    )(h, gyc, gb, *os_, *ls_, p, *weights)


def kernel(x, p, positions, g_ffn1, w_ffn1_gate, w_ffn1_up, w_ffn1_down, g_mix, w_in,
           w_dw, b_dw, g_conv_ln, b_conv_ln, w_conv_out, w_attn_out, w_mix_out,
           g_ffn2, w_ffn2_gate, w_ffn2_up, w_ffn2_down, g_ple, w_ple_gate, w_ple_proj,
           g_final):
    b, s, d = x.shape
    depth = p.shape[0]
    t = b * s
    c_conv = w_dw.shape[-1]
    assert w_in.shape[-1] == 2 * c_conv + 3 * ATTN_W + 2 * d
    tm, tm_mix = 512, 256
    max_dil = max(dil for _, dil in GROUPS)
    assert s % tm == 0 and s % tm_mix == 0
    assert tm_mix % CONV_ROWS == 0 and (tm_mix // max_dil) % 16 == 0
    for window, dil in GROUPS:
        assert window // dil == Q_BLOCK and (s // dil) % Q_BLOCK == 0

    assert depth == 1
    i = 0
    row = lambda v: v.reshape(1, -1)
    cos, sin, (wg1, wu1, wd1) = _rope_tables(
        positions, ROPE_ROWS, [w_ffn1_gate[i], w_ffn1_up[i], w_ffn1_down[i]])
    h, u, (win, wco, wao, wmix, wg2, wu2, wd2, wpg, wpp) = _ffn1(
        x.reshape(t, d), row(g_ffn1[i]), wg1, wu1, wd1, row(g_mix[i]), tm,
        [w_in[i], w_conv_out[i], w_attn_out[i], w_mix_out[i], w_ffn2_gate[i], w_ffn2_up[i],
         w_ffn2_down[i], w_ple_gate[i], w_ple_proj[i]])
    gyc, q0, q1, q2, k0, k1, k2, v0, v1, v2, gb = _mixer_in(
        u, win, cos, sin, w_dw[i], row(b_dw[i]), row(g_conv_ln[i]), row(b_conv_ln[i]), wco,
        b, tm_mix)
    os_, ls_ = [], []
    for (_, dil), q, k, v in zip(GROUPS, (q0, q1, q2), (k0, k1, k2), (v0, v1, v2)):
        o, lse = _attention_group(q, k, v, b, dil, ATTN_ROWS, max(1, ATTN_ROWS * dil // s))
        os_.append(o)
        ls_.append(lse)
    h = _tail(h, gyc, gb, os_, ls_, p[i].reshape(t, -1), wao, wmix, row(g_ffn2[i]), wg2, wu2,
              wd2, row(g_ple[i]), wpg, wpp, row(g_final), tm, True)
    return h.reshape(b, s, d)
```

```python
import functools

import jax
import jax.numpy as jnp
from jax import lax
from jax.experimental import pallas as pl
from jax.experimental.pallas import tpu as pltpu

HEAD_DIM = 128
HEADS_PER_GROUP = 4
GROUPS = ((128, 1), (512, 4), (2048, 16))
N_GROUPS = len(GROUPS)
GROUP_W = HEADS_PER_GROUP * HEAD_DIM
ATTN_W = N_GROUPS * GROUP_W
CONV_WIDTH = 31
Q_BLOCK = 128
ROPE_THETA = 10000.0
EPS = 1e-6
NEG = -1e30
LOG2_E = 1.4426950408889634

LANES = 128
SUBLANES = 8
BF16_ROWS = 16
MXU_COLS = 256
CONV_HALO = 32
CONV_ROWS = 128
ATTN_UNROLL = 4
ATTN_ROWS = 2048
ROPE_ROWS = 1024
FFN1_PARTS = 2
TAIL_PARTS = 1
STAT_REP = LANES // HEADS_PER_GROUP
VMEM_LIMIT = 56 * 1024 * 1024

BF16 = jnp.bfloat16
F32 = jnp.float32


def _cparams(n_axes):
    return pltpu.CompilerParams(
        dimension_semantics=("arbitrary",) * n_axes, vmem_limit_bytes=VMEM_LIMIT)


def _resident(shape):
    nd = len(shape)
    return pl.BlockSpec(shape, lambda *_: (0,) * nd, pipeline_mode=pl.Buffered(1))


def _rms(x, g):
    ms = jnp.mean(x * x, axis=-1, keepdims=True)
    return x * lax.rsqrt(ms + EPS) * g


def _sigmoid(x):
    return 1.0 / (1.0 + jnp.exp(-x))


def _dot(a, b):
    return jnp.dot(a, b, preferred_element_type=F32)


def _swiglu_half_step(x, g, wg_ref, wu_ref, wd_ref):
    xn = _rms(x, g).astype(BF16)
    ff = wg_ref.shape[1]
    acts = []
    for lo in range(0, ff, MXU_COLS):
        gate = _dot(xn, wg_ref[:, lo:lo + MXU_COLS])
        up = _dot(xn, wu_ref[:, lo:lo + MXU_COLS])
        acts.append((gate * _sigmoid(gate) * up).astype(BF16))
    act = jnp.concatenate(acts, axis=1)
    return x + 0.5 * _dot(act, wd_ref[...])


def _cast_plan(weights, steps):
    in_specs, out_specs, out_shape = [], [], []
    for w in weights:
        rows, cols = w.shape
        n = max(k for k in range(1, steps + 1)
                if rows % k == 0 and (rows // k) % BF16_ROWS == 0)
        spec = pl.BlockSpec((rows // n, cols), lambda i, n=n: (jnp.minimum(i, n - 1), 0))
        in_specs.append(spec)
        out_specs.append(spec)
        out_shape.append(jax.ShapeDtypeStruct((rows, cols), BF16))
    return in_specs, out_specs, out_shape


def _cast_slabs(src_refs, dst_refs):
    for src, dst in zip(src_refs, dst_refs):
        dst[...] = src[...].astype(BF16)


def _rope_kernel(pos_lo_ref, pos_hi_ref, freq_ref, *refs):
    n_cast = (len(refs) - 2) // 2
    cos_ref, sin_ref = refs[n_cast:n_cast + 2]
    lane = lax.broadcasted_iota(jnp.int32, cos_ref.shape, 1)
    pos = jnp.where(lane < HEAD_DIM // 2, pos_lo_ref[...], pos_hi_ref[...])
    ang = pos.astype(F32) * freq_ref[...]
    cos_ref[...] = jnp.cos(ang)
    sin_ref[...] = jnp.sin(ang)
    _cast_slabs(refs[:n_cast], refs[n_cast + 2:])


def _rope_tables(positions, tm, cast_weights):
    t = positions.size
    half_t = t // 2
    inv_freq = ROPE_THETA ** (-jnp.arange(0, HEAD_DIM, 2, dtype=F32) / HEAD_DIM)
    freq = jnp.concatenate([inv_freq, inv_freq])[None, :]
    pos = positions.reshape(2, half_t, 1)
    tile = pl.BlockSpec((tm, HEAD_DIM), lambda i: (i, 0))
    col = pl.BlockSpec((tm, 1), lambda i: (i, 0))
    steps = half_t // tm
    c_in, c_out, c_shape = _cast_plan(cast_weights, steps)
    outs = pl.pallas_call(
        _rope_kernel,
        grid=(steps,),
        in_specs=[col, col, pl.BlockSpec((1, HEAD_DIM), lambda i: (0, 0))] + c_in,
        out_specs=[tile, tile] + c_out,
        out_shape=[jax.ShapeDtypeStruct((half_t, HEAD_DIM), F32)] * 2 + c_shape,
        compiler_params=_cparams(1),
        name="rope_tables",
    )(pos[0], pos[1], freq, *cast_weights)
    return outs[0], outs[1], outs[2:]


def _unpack_rope(cos_ref, sin_ref, upper_half):
    half = HEAD_DIM // 2
    lane = lax.broadcasted_iota(jnp.int32, cos_ref.shape, 1)
    low = lane < half
    own = jnp.logical_xor(low, upper_half)
    c, s = cos_ref[...], sin_ref[...]
    cos = jnp.where(own, c, pltpu.roll(c, half, axis=1))
    sin = jnp.where(own, s, pltpu.roll(s, half, axis=1))
    return cos, jnp.where(low, -sin, sin)


def _ffn1_kernel(x_ref, g1_ref, wg_ref, wu_ref, wd_ref, gmix_ref, *refs):
    n_cast = (len(refs) - 2) // 2
    h_ref, u_ref = refs[n_cast:n_cast + 2]
    n = x_ref.shape[0] // FFN1_PARTS
    for part in range(FFN1_PARTS):
        rows = slice(part * n, (part + 1) * n)
        h = _swiglu_half_step(x_ref[rows, :], g1_ref[...], wg_ref, wu_ref, wd_ref)
        h_ref[rows, :] = h
        u_ref[rows, :] = _rms(h, gmix_ref[...]).astype(BF16)
    _cast_slabs(refs[:n_cast], refs[n_cast + 2:])


def _ffn1(x, g1, wg, wu, wd, gmix, tm, cast_weights):
    t, d = x.shape
    ff = wg.shape[1]
    tile = pl.BlockSpec((tm, d), lambda i: (i, 0))
    steps = t // tm
    c_in, c_out, c_shape = _cast_plan(cast_weights, steps)
    outs = pl.pallas_call(
        _ffn1_kernel,
        grid=(steps,),
        in_specs=[tile, _resident((1, d)), _resident((d, ff)), _resident((d, ff)),
                  _resident((ff, d)), _resident((1, d))] + c_in,
        out_specs=[tile, tile] + c_out,
        out_shape=[jax.ShapeDtypeStruct((t, d), F32), jax.ShapeDtypeStruct((t, d), BF16)] + c_shape,
        compiler_params=_cparams(1),
        name="ffn1",
    )(x, g1, wg, wu, wd, gmix, *cast_weights)
    return outs[0], outs[1], outs[2:]


def _rotary(t, cos, sin_signed):
    return t * cos + pltpu.roll(t, HEAD_DIM // 2, axis=1) * sin_signed


CONV_OFF = CONV_HALO - (CONV_WIDTH - 1)
CONV_SLABS = 4
CONV_MXU_PHASES = (2, 3, 4, 5, 6, 7)


def _build_conv_diag(w_ref, dmat):
    eye = (lax.broadcasted_iota(jnp.int32, (LANES, LANES), 0)
           == lax.broadcasted_iota(jnp.int32, (LANES, LANES), 1))
    for c in range(dmat.shape[0]):
        for q in range(CONV_SLABS):
            for g, r in enumerate(CONV_MXU_PHASES):
                j = SUBLANES * q + r - CONV_OFF
                assert 0 <= j < CONV_WIDTH
                tap = w_ref[j:j + 1, c * LANES:(c + 1) * LANES]
                dmat[c, q * LANES:(q + 1) * LANES, g * LANES:(g + 1) * LANES] = (
                    jnp.where(eye, tap, 0.0).astype(BF16))


def _conv_lane_chunk(xbuf, w_ref, b_ref, dmat, acc_ref, tm, c):
    cols = slice(c * LANES, (c + 1) * LANES)
    taps = [w_ref[j:j + 1, cols] for j in range(CONV_WIDTH)]
    bias = b_ref[:, cols]
    m_rows = tm + BF16_ROWS
    slabs = [xbuf[SUBLANES * q:SUBLANES * q + m_rows, cols].astype(BF16) for q in range(CONV_SLABS)]
    phase_sums = _dot(jnp.concatenate(slabs, axis=1), dmat[c])
    for rb in range(tm // CONV_ROWS):
        base = rb * CONV_ROWS
        out = None
        for r in range(SUBLANES):
            if r in CONV_MXU_PHASES:
                g = CONV_MXU_PHASES.index(r)
                shifted = phase_sums[base + r:base + r + CONV_ROWS, g * LANES:(g + 1) * LANES]
            else:
                n = CONV_ROWS if r == 0 else CONV_ROWS + SUBLANES
                part = None
                for q in range((CONV_OFF + CONV_WIDTH - 1) // SUBLANES + 1):
                    j = SUBLANES * q + r - CONV_OFF
                    if 0 <= j < CONV_WIDTH:
                        lo = base + SUBLANES * q
                        term = taps[j] * xbuf[lo:lo + n, cols]
                        part = term if part is None else part + term
                shifted = part if r == 0 else part[r:r + CONV_ROWS]
            out = shifted if out is None else out + shifted
        acc_ref[base:base + CONV_ROWS, cols] = out + bias


def _mixer_in_kernel(u_ref, w_ref, cos_ref, sin_ref, wdw_ref, bdw_ref, gln_ref, bln_ref, wco_ref,
                     gyc_ref, q0_ref, q1_ref, q2_ref, k0_ref, k1_ref, k2_ref,
                     v0_ref, v1_ref, v2_ref, gb_ref, xbuf, acc_ref, ga_buf, rope_buf, perm, dmat,
                     *, tm, c_conv, d_model):
    dils = [dil for _, dil in GROUPS]

    @pl.when((pl.program_id(0) == 0) & (pl.program_id(1) == 0))
    def _():
        _build_conv_diag(wdw_ref, dmat)
        xbuf[CONV_HALO + tm:, :] = jnp.zeros((BF16_ROWS, c_conv), F32)
        dst = lax.broadcasted_iota(jnp.int32, (tm, tm), 0)
        tok = lax.broadcasted_iota(jnp.int32, (tm, tm), 1)
        for g, dil in enumerate(dils):
            if dil > 1:
                n = tm // dil
                perm[g] = jnp.where(tok == (dst % n) * dil + dst // n, 1.0, 0.0).astype(BF16)

    @pl.when(pl.program_id(1) == 0)
    def _():
        xbuf[0:CONV_HALO, :] = jnp.zeros((CONV_HALO, c_conv), F32)

    u = u_ref[...]

    def proj(lo, width, lhs=None):
        return _dot(u if lhs is None else lhs, w_ref[:, lo:lo + width])

    a = proj(0, c_conv)
    gate = proj(c_conv, c_conv)
    xbuf[CONV_HALO:CONV_HALO + tm, :] = a * _sigmoid(gate)

    base = 2 * c_conv
    gbase = base + 3 * ATTN_W

    tile_idx = pl.program_id(0) * pl.num_programs(1) + pl.program_id(1)
    n_tiles = pl.num_programs(0) * pl.num_programs(1)
    cos, sin = _unpack_rope(cos_ref, sin_ref, 2 * tile_idx >= n_tiles)
    rope_buf[0] = cos
    rope_buf[1] = sin

    def strided_rows(g):
        dil = dils[g]
        if dil == 1:
            return u, cos, sin
        n = tm // dil
        u_g = _dot(perm[g], u).astype(BF16)
        cos_g = jnp.concatenate([rope_buf[0, pl.ds(r, n, stride=dil), :] for r in range(dil)], axis=0)
        sin_g = jnp.concatenate([rope_buf[1, pl.ds(r, n, stride=dil), :] for r in range(dil)], axis=0)
        return u_g, cos_g, sin_g

    def emit(out_ref, val, dil):
        n = tm // dil
        for r in range(dil):
            out_ref[:, r * GROUP_W:(r + 1) * GROUP_W] = val[r * n:(r + 1) * n, :].astype(BF16)

    def group_pieces(g, q_ref, k_ref, v_ref):
        dil = dils[g]
        cache = []

        def inputs():
            if not cache:
                cache.append(strided_rows(g))
            return cache[0]

        def rotary_piece(out_ref, lo, scale):
            def piece():
                u_g, cos_g, sin_g = inputs()
                t = proj(lo, GROUP_W, u_g)
                heads = [_rotary(t[:, h * HEAD_DIM:(h + 1) * HEAD_DIM], cos_g, sin_g)
                         for h in range(HEADS_PER_GROUP)]
                if scale is not None:
                    heads = [x * scale for x in heads]
                emit(out_ref, jnp.concatenate(heads, axis=1), dil)
            return piece

        def value_piece():
            emit(v_ref, proj(base + 2 * ATTN_W + g * GROUP_W, GROUP_W, inputs()[0]), dil)

        return [rotary_piece(q_ref, base + g * GROUP_W, HEAD_DIM ** -0.5 * LOG2_E),
                rotary_piece(k_ref, base + ATTN_W + g * GROUP_W, None), value_piece]

    def gate_a_piece():
        ga_buf[...] = _sigmoid(proj(gbase, d_model))

    def gate_b_piece():
        gb_ref[...] = _sigmoid(proj(gbase + d_model, d_model)).astype(BF16)

    pieces = [gate_a_piece, gate_b_piece]
    for g, refs in enumerate(((q0_ref, k0_ref, v0_ref), (q1_ref, k1_ref, v1_ref),
                              (q2_ref, k2_ref, v2_ref))):
        pieces += group_pieces(g, *refs)

    for c in range(c_conv // LANES):
        _conv_lane_chunk(xbuf, wdw_ref, bdw_ref, dmat, acc_ref, tm, c)
    xbuf[0:CONV_HALO, :] = xbuf[tm:tm + CONV_HALO, :]
    pieces[0]()

    acc = acc_ref[...]
    mu = jnp.mean(acc, axis=-1, keepdims=True)
    cen = acc - mu
    var = jnp.mean(cen * cen, axis=-1, keepdims=True)
    y = cen * lax.rsqrt(var + EPS) * gln_ref[...] + bln_ref[...]
    y = (y * _sigmoid(y)).astype(BF16)
    gyc_ref[...] = (ga_buf[...] * _dot(y, wco_ref[...])).astype(BF16)

    for piece in pieces[1:]:
        piece()


def _mixer_in(u, w_in, cos, sin, w_dw, b_dw, g_ln, b_ln, w_co, batch, tm):
    t, d = u.shape
    in_w = w_in.shape[1]
    c_conv = w_dw.shape[1]
    per_seq = t // batch // tm
    idx = lambda b, i: (b * per_seq + i, 0)
    row = lambda w: pl.BlockSpec((tm, w), idx)
    half_tiles = t // 2 // tm
    rope = pl.BlockSpec((tm, HEAD_DIM), lambda b, i: (lax.rem(b * per_seq + i, half_tiles), 0))
    strided = lambda dil: pl.BlockSpec((tm // dil, dil * GROUP_W), idx)
    strided_sds = lambda dil: jax.ShapeDtypeStruct((t // dil, dil * GROUP_W), BF16)
    dils = [dil for _, dil in GROUPS]
    out_specs = [row(d)] + [strided(dil) for _ in range(3) for dil in dils] + [row(d)]
    out_shape = ([jax.ShapeDtypeStruct((t, d), BF16)]
                 + [strided_sds(dil) for _ in range(3) for dil in dils]
                 + [jax.ShapeDtypeStruct((t, d), BF16)])
    return pl.pallas_call(
        functools.partial(_mixer_in_kernel, tm=tm, c_conv=c_conv, d_model=d),
        grid=(batch, per_seq),
        in_specs=[row(d), _resident((d, in_w)), rope, rope,
                  _resident((CONV_WIDTH, c_conv)), _resident((1, c_conv)), _resident((1, c_conv)),
                  _resident((1, c_conv)), _resident((c_conv, d))],
        out_specs=out_specs,
        out_shape=out_shape,
        scratch_shapes=[pltpu.VMEM((CONV_HALO + tm + BF16_ROWS, c_conv), F32),
                        pltpu.VMEM((tm, c_conv), F32),
                        pltpu.VMEM((tm, d), F32),
                        pltpu.VMEM((2, tm, HEAD_DIM), F32),
                        pltpu.VMEM((N_GROUPS, tm, tm), BF16),
                        pltpu.VMEM((c_conv // LANES, CONV_SLABS * LANES,
                                    len(CONV_MXU_PHASES) * LANES), BF16)],
        compiler_params=_cparams(2),
        name="mixer_in",
    )(u, w_in, cos, sin, w_dw, b_dw, g_ln, b_ln, w_co)


def _attn_kernel(q_ref, k_ref, v_ref, kh_ref, vh_ref, o_ref, lse_ref, kbuf, vbuf,
                 *, lt, unroll, rps):
    first_tile = pl.program_id(2) == 0
    ones = jnp.ones((lt + Q_BLOCK, HEAD_DIM), BF16)
    for rr in range(rps):
        kbuf[rr, 0:Q_BLOCK, :] = kh_ref[0, :, rr * GROUP_W:(rr + 1) * GROUP_W]
        kbuf[rr, Q_BLOCK:, :] = k_ref[0, :, rr * GROUP_W:(rr + 1) * GROUP_W]
        for h in range(HEADS_PER_GROUP):
            src = slice(rr * GROUP_W + h * HEAD_DIM, rr * GROUP_W + (h + 1) * HEAD_DIM)
            vbuf[rr, 0:Q_BLOCK, 2 * h * HEAD_DIM:(2 * h + 1) * HEAD_DIM] = vh_ref[0, :, src]
            vbuf[rr, Q_BLOCK:, 2 * h * HEAD_DIM:(2 * h + 1) * HEAD_DIM] = v_ref[0, :, src]
            vbuf[rr, :, (2 * h + 1) * HEAD_DIM:(2 * h + 2) * HEAD_DIM] = ones

    qi = lax.broadcasted_iota(jnp.int32, (Q_BLOCK, 2 * Q_BLOCK), 0)
    ki = lax.broadcasted_iota(jnp.int32, (Q_BLOCK, 2 * Q_BLOCK), 1)
    dist = qi + Q_BLOCK - ki
    band = (dist >= 0) & (dist <= Q_BLOCK)
    lane = lax.broadcasted_iota(jnp.int32, (Q_BLOCK, LANES), 1)
    den_lane = lane % STAT_REP >= STAT_REP // 2
    nt = (((1,), (1,)), ((), ()))

    def blocks(units):
        work = [(rr, start, mask, h) for rr, start, mask in units for h in range(HEADS_PER_GROUP)]
        col = lambda rr, h: slice(rr * GROUP_W + h * HEAD_DIM, rr * GROUP_W + (h + 1) * HEAD_DIM)
        s = [lax.dot_general(q_ref[0, pl.ds(start, Q_BLOCK), col(rr, h)],
                             kbuf[rr, pl.ds(start, 2 * Q_BLOCK), col(0, h)], nt,
                             preferred_element_type=F32) for rr, start, _, h in work]
        s = [jnp.where(mask, x, NEG) for x, (_, _, mask, _) in zip(s, work)]
        m = [jnp.max(x, axis=-1, keepdims=True) for x in s]
        e = [jnp.exp2(x - mx).astype(BF16) for x, mx in zip(s, m)]
        pv = [_dot(ex, vbuf[rr, pl.ds(start, 2 * Q_BLOCK), 2 * h * HEAD_DIM:(2 * h + 2) * HEAD_DIM])
              for ex, (rr, start, _, h) in zip(e, work)]
        for u, (rr, start, _) in enumerate(units):
            rows = pl.ds(start, Q_BLOCK)
            packed = None
            for h in range(HEADS_PER_GROUP):
                i = u * HEADS_PER_GROUP + h
                den = pv[i][:, HEAD_DIM:]
                o_ref[0, rows, col(rr, h)] = pv[i][:, :HEAD_DIM].astype(BF16)
                stat = jnp.where(den_lane, den, m[i])
                packed = stat if packed is None else jnp.where(lane >= h * STAT_REP, stat, packed)
            lse_ref[0, rows, rr * LANES:(rr + 1) * LANES] = packed

    def run(units):
        for lo in range(0, len(units), unroll):
            blocks(units[lo:lo + unroll])

    per = lt // Q_BLOCK
    upr = min(unroll, per)
    first_mask = band & jnp.logical_or(ki >= Q_BLOCK, jnp.logical_not(first_tile))
    run([(rr, j * Q_BLOCK, first_mask if j == 0 else band)
         for rr in range(rps) for j in range(upr)])

    def body(n, carry):
        start = n * (upr * Q_BLOCK)
        run([(rr, pl.multiple_of(start + j * Q_BLOCK, Q_BLOCK), band)
             for rr in range(rps) for j in range(upr)])
        return carry

    lax.fori_loop(1, per // upr, body, 0)


def _attention_group(q, k, v, batch, dilation, lt, rps):
    rows, _ = q.shape
    l = rows // batch
    lt = min(lt, l)
    per = lt // Q_BLOCK
    unroll = min(ATTN_UNROLL, per * rps)
    assert l % lt == 0 and dilation % rps == 0 and per % min(unroll, per) == 0
    view = lambda x: x.reshape(batch, l, x.shape[-1])
    tile = lambda width: pl.BlockSpec((1, lt, rps * width), lambda bi, r, i: (bi, i, r))
    halo = pl.BlockSpec((1, Q_BLOCK, rps * GROUP_W),
                        lambda bi, r, i: (bi, jnp.maximum(i * per - 1, 0), r))
    o, lse = pl.pallas_call(
        functools.partial(_attn_kernel, lt=lt, unroll=unroll, rps=rps),
        grid=(batch, dilation // rps, l // lt),
        in_specs=[tile(GROUP_W), tile(GROUP_W), tile(GROUP_W), halo, halo],
        out_specs=[tile(GROUP_W), tile(LANES)],
        out_shape=[jax.ShapeDtypeStruct((batch, l, dilation * GROUP_W), BF16),
                   jax.ShapeDtypeStruct((batch, l, dilation * LANES), F32)],
        scratch_shapes=[pltpu.VMEM((rps, lt + Q_BLOCK, GROUP_W), BF16),
                        pltpu.VMEM((rps, lt + Q_BLOCK, 2 * GROUP_W), BF16)],
        compiler_params=_cparams(3),
        name=f"attn_d{dilation}",
    )(view(q), view(k), view(v), view(k), view(v))
    return o.reshape(rows, dilation * GROUP_W), lse.reshape(rows, dilation * LANES)


def _tail_kernel(h_ref, gyc_ref, gb_ref, o0_ref, o1_ref, o2_ref, l0_ref, l1_ref, l2_ref, p_ref,
                 wao_ref, wmix_ref, g2_ref, wg_ref, wu_ref, wd_ref, gple_ref, wpg_ref, wpp_ref,
                 gfin_ref, y_ref, obuf, lbuf, spread, *, tm, final_norm):
    @pl.when(pl.program_id(0) == 0)
    def _():
        src = lax.broadcasted_iota(jnp.int32, spread.shape, 0) % LANES
        dst_head = lax.broadcasted_iota(jnp.int32, spread.shape, 1) // HEAD_DIM
        spread[...] = jnp.where(src == dst_head * STAT_REP, 1.0, 0.0).astype(BF16)

    parts = [_mixer_out_rows(part, tm // TAIL_PARTS, h_ref, gyc_ref, gb_ref,
                             (o0_ref, o1_ref, o2_ref), (l0_ref, l1_ref, l2_ref),
                             wao_ref, wmix_ref, obuf, lbuf, spread) for part in range(TAIL_PARTS)]
    h = parts[0] if TAIL_PARTS == 1 else jnp.concatenate(parts, axis=0)
    h = _swiglu_half_step(h, g2_ref[...], wg_ref, wu_ref, wd_ref)
    gate = _sigmoid(_dot(_rms(h, gple_ref[...]).astype(BF16), wpg_ref[...]))
    h = h + gate * _dot(p_ref[...].astype(BF16), wpp_ref[...])
    y_ref[...] = _rms(h, gfin_ref[...]) if final_norm else h


def _mixer_out_rows(part, tp, h_ref, gyc_ref, gb_ref, o_refs, l_refs, wao_ref, wmix_ref,
                    obuf, lbuf, spread):
    rows_tok = slice(part * tp, (part + 1) * tp)

    def token_order(dst, src_ref, dil, width):
        chunks = width // LANES
        n = tp // dil
        src_rows = slice(part * n, (part + 1) * n)
        if dil == 1:
            return [src_ref[src_rows, c * LANES:(c + 1) * LANES].astype(F32) for c in range(chunks)]
        for r in range(dil):
            for c in range(chunks):
                lo = r * width + c * LANES
                dst[c, pl.ds(part * tp + r, n, stride=dil), :] = (
                    src_ref[src_rows, lo:lo + LANES].astype(F32))
        return [dst[c, rows_tok, :] for c in range(chunks)]

    dils = [dil for _, dil in GROUPS]
    ls = [token_order(lbuf.at[g], ref, dils[g], LANES)[0] for g, ref in enumerate(l_refs)]
    os_ = [token_order(obuf.at[g], ref, dils[g], GROUP_W) for g, ref in enumerate(o_refs)]

    half = STAT_REP // 2
    lane = lax.broadcasted_iota(jnp.int32, ls[0].shape, 1)
    m = jnp.maximum(jnp.maximum(ls[0], ls[1]), ls[2])
    es = [jnp.exp2(x - m) for x in ls]
    dens = [pltpu.roll(x, LANES - half, axis=1) for x in ls]
    inv = 1.0 / (es[0] * dens[0] + es[1] * dens[1] + es[2] * dens[2])
    alphas = [jnp.where(lane % STAT_REP < half, x * inv, 0.0) for x in es]
    wide = []
    for a in alphas:
        hi = a.astype(BF16)
        lo = (a - hi.astype(F32)).astype(BF16)
        wide.append(_dot(jnp.concatenate([hi, lo], axis=1), spread[...]))
    heads = []
    for h in range(HEADS_PER_GROUP):
        acc = None
        for g in range(N_GROUPS):
            term = wide[g][:, h * HEAD_DIM:(h + 1) * HEAD_DIM] * os_[g][h]
            acc = term if acc is None else acc + term
        heads.append(acc.astype(BF16))
    o = jnp.concatenate(heads, axis=1)
    y_attn = _dot(o, wao_ref[...])
    mixed = gyc_ref[rows_tok, :].astype(F32) + gb_ref[rows_tok, :].astype(F32) * y_attn
    return h_ref[rows_tok, :] + _dot(mixed.astype(BF16), wmix_ref[...])


def _tail(h, gyc, gb, os_, ls_, p, wao, wmix, g2, wg, wu, wd, gple, wpg, wpp, gfin, tm,
          final_norm):
    t, d = h.shape
    row = lambda w: pl.BlockSpec((tm, w), lambda i: (i, 0))
    strided = lambda dil, w: pl.BlockSpec((tm // dil, dil * w), lambda i: (i, 0))
    res = lambda x: _resident(x.shape)
    weights = (wao, wmix, g2, wg, wu, wd, gple, wpg, wpp, gfin)
    dils = [dil for _, dil in GROUPS]
    return pl.pallas_call(
        functools.partial(_tail_kernel, tm=tm, final_norm=final_norm),
        grid=(t // tm,),
        in_specs=[row(d), row(d), row(d)] + [strided(dil, GROUP_W) for dil in dils]
                 + [strided(dil, LANES) for dil in dils]
                 + [row(p.shape[1])] + [res(x) for x in weights],
        out_specs=row(d),
        out_shape=jax.ShapeDtypeStruct((t, d), F32),
        scratch_shapes=[pltpu.VMEM((N_GROUPS, GROUP_W // LANES, tm, LANES), F32),
                        pltpu.VMEM((N_GROUPS, 1, tm, LANES), F32),
                        pltpu.VMEM((2 * LANES, GROUP_W), BF16)],
        compiler_params=_cparams(1),
        name="tail",
    )(h, gyc, gb, *os_, *ls_, p, *weights)


def kernel(x, p, positions, g_ffn1, w_ffn1_gate, w_ffn1_up, w_ffn1_down, g_mix, w_in,
           w_dw, b_dw, g_conv_ln, b_conv_ln, w_conv_out, w_attn_out, w_mix_out,
           g_ffn2, w_ffn2_gate, w_ffn2_up, w_ffn2_down, g_ple, w_ple_gate, w_ple_proj,
           g_final):
    b, s, d = x.shape
    depth = p.shape[0]
    t = b * s
    c_conv = w_dw.shape[-1]
    assert w_in.shape[-1] == 2 * c_conv + 3 * ATTN_W + 2 * d
    tm, tm_mix = 512, 256
    max_dil = max(dil for _, dil in GROUPS)
    assert s % tm == 0 and s % tm_mix == 0
    assert tm_mix % CONV_ROWS == 0 and (tm_mix // max_dil) % 16 == 0
    for window, dil in GROUPS:
        assert window // dil == Q_BLOCK and (s // dil) % Q_BLOCK == 0

    assert depth == 1
    i = 0
    row = lambda v: v.reshape(1, -1)
    cos, sin, (wg1, wu1, wd1) = _rope_tables(
        positions, ROPE_ROWS, [w_ffn1_gate[i], w_ffn1_up[i], w_ffn1_down[i]])
    h, u, (win, wco, wao, wmix, wg2, wu2, wd2, wpg, wpp) = _ffn1(
        x.reshape(t, d), row(g_ffn1[i]), wg1, wu1, wd1, row(g_mix[i]), tm,
        [w_in[i], w_conv_out[i], w_attn_out[i], w_mix_out[i], w_ffn2_gate[i], w_ffn2_up[i],
         w_ffn2_down[i], w_ple_gate[i], w_ple_proj[i]])
    gyc, q0, q1, q2, k0, k1, k2, v0, v1, v2, gb = _mixer_in(
        u, win, cos, sin, w_dw[i], row(b_dw[i]), row(g_conv_ln[i]), row(b_conv_ln[i]), wco,
        b, tm_mix)
    os_, ls_ = [], []
    for (_, dil), q, k, v in zip(GROUPS, (q0, q1, q2), (k0, k1, k2), (v0, v1, v2)):
        o, lse = _attention_group(q, k, v, b, dil, ATTN_ROWS, max(1, ATTN_ROWS * dil // s))
        os_.append(o)
        ls_.append(lse)
    h = _tail(h, gyc, gb, os_, ls_, p[i].reshape(t, -1), wao, wmix, row(g_ffn2[i]), wg2, wu2,
              wd2, row(g_ple[i]), wpg, wpp, row(g_final), tm, True)
    return h.reshape(b, s, d)
```

```python
import functools

import jax
import jax.numpy as jnp
from jax import lax
from jax.experimental import pallas as pl
from jax.experimental.pallas import tpu as pltpu

HEAD_DIM = 128
HEADS_PER_GROUP = 4
GROUPS = ((128, 1), (512, 4), (2048, 16))
N_GROUPS = len(GROUPS)
GROUP_W = HEADS_PER_GROUP * HEAD_DIM
ATTN_W = N_GROUPS * GROUP_W
CONV_WIDTH = 31
Q_BLOCK = 128
ROPE_THETA = 10000.0
EPS = 1e-6
NEG = -1e30
LOG2_E = 1.4426950408889634

LANES = 128
SUBLANES = 8
BF16_ROWS = 16
MXU_COLS = 256
CONV_HALO = 32
CONV_ROWS = 128
ATTN_UNROLL = 4
ATTN_ROWS = 2048
ROPE_ROWS = 1024
FFN1_PARTS = 2
TAIL_PARTS = 1
STAT_REP = LANES // HEADS_PER_GROUP
VMEM_LIMIT = 56 * 1024 * 1024

BF16 = jnp.bfloat16
F32 = jnp.float32


def _cparams(n_axes):
    return pltpu.CompilerParams(
        dimension_semantics=("arbitrary",) * n_axes, vmem_limit_bytes=VMEM_LIMIT)


def _resident(shape):
    nd = len(shape)
    return pl.BlockSpec(shape, lambda *_: (0,) * nd, pipeline_mode=pl.Buffered(1))


def _rms(x, g):
    ms = jnp.mean(x * x, axis=-1, keepdims=True)
    return x * lax.rsqrt(ms + EPS) * g


def _sigmoid(x):
    return 1.0 / (1.0 + jnp.exp(-x))


def _dot(a, b):
    return jnp.dot(a, b, preferred_element_type=F32)


def _swiglu_half_step(x, g, wg_ref, wu_ref, wd_ref):
    xn = _rms(x, g).astype(BF16)
    ff = wg_ref.shape[1]
    acts = []
    for lo in range(0, ff, MXU_COLS):
        gate = _dot(xn, wg_ref[:, lo:lo + MXU_COLS])
        up = _dot(xn, wu_ref[:, lo:lo + MXU_COLS])
        acts.append((gate * _sigmoid(gate) * up).astype(BF16))
    act = jnp.concatenate(acts, axis=1)
    return x + 0.5 * _dot(act, wd_ref[...])


def _cast_plan(weights, steps):
    in_specs, out_specs, out_shape = [], [], []
    for w in weights:
        rows, cols = w.shape
        n = max(k for k in range(1, steps + 1)
                if rows % k == 0 and (rows // k) % BF16_ROWS == 0)
        spec = pl.BlockSpec((rows // n, cols), lambda i, n=n: (jnp.minimum(i, n - 1), 0))
        in_specs.append(spec)
        out_specs.append(spec)
        out_shape.append(jax.ShapeDtypeStruct((rows, cols), BF16))
    return in_specs, out_specs, out_shape


def _cast_slabs(src_refs, dst_refs):
    for src, dst in zip(src_refs, dst_refs):
        dst[...] = src[...].astype(BF16)


def _rope_kernel(pos_lo_ref, pos_hi_ref, freq_ref, *refs):
    n_cast = (len(refs) - 2) // 2
    cos_ref, sin_ref = refs[n_cast:n_cast + 2]
    lane = lax.broadcasted_iota(jnp.int32, cos_ref.shape, 1)
    pos = jnp.where(lane < HEAD_DIM // 2, pos_lo_ref[...], pos_hi_ref[...])
    ang = pos.astype(F32) * freq_ref[...]
    cos_ref[...] = jnp.cos(ang)
    sin_ref[...] = jnp.sin(ang)
    _cast_slabs(refs[:n_cast], refs[n_cast + 2:])


def _rope_tables(positions, tm, cast_weights):
    t = positions.size
    half_t = t // 2
    inv_freq = ROPE_THETA ** (-jnp.arange(0, HEAD_DIM, 2, dtype=F32) / HEAD_DIM)
    freq = jnp.concatenate([inv_freq, inv_freq])[None, :]
    pos = positions.reshape(2, half_t, 1)
    tile = pl.BlockSpec((tm, HEAD_DIM), lambda i: (i, 0))
    col = pl.BlockSpec((tm, 1), lambda i: (i, 0))
    steps = half_t // tm
    c_in, c_out, c_shape = _cast_plan(cast_weights, steps)
    outs = pl.pallas_call(
        _rope_kernel,
        grid=(steps,),
        in_specs=[col, col, pl.BlockSpec((1, HEAD_DIM), lambda i: (0, 0))] + c_in,
        out_specs=[tile, tile] + c_out,
        out_shape=[jax.ShapeDtypeStruct((half_t, HEAD_DIM), F32)] * 2 + c_shape,
        compiler_params=_cparams(1),
        name="rope_tables",
    )(pos[0], pos[1], freq, *cast_weights)
    return outs[0], outs[1], outs[2:]


def _unpack_rope(cos_ref, sin_ref, upper_half):
    half = HEAD_DIM // 2
    lane = lax.broadcasted_iota(jnp.int32, cos_ref.shape, 1)
    low = lane < half
    own = jnp.logical_xor(low, upper_half)
    c, s = cos_ref[...], sin_ref[...]
    cos = jnp.where(own, c, pltpu.roll(c, half, axis=1))
    sin = jnp.where(own, s, pltpu.roll(s, half, axis=1))
    return cos, jnp.where(low, -sin, sin)


def _ffn1_kernel(x_ref, g1_ref, wg_ref, wu_ref, wd_ref, gmix_ref, *refs):
    n_cast = (len(refs) - 2) // 2
    h_ref, u_ref = refs[n_cast:n_cast + 2]
    n = x_ref.shape[0] // FFN1_PARTS
    for part in range(FFN1_PARTS):
        rows = slice(part * n, (part + 1) * n)
        h = _swiglu_half_step(x_ref[rows, :], g1_ref[...], wg_ref, wu_ref, wd_ref)
        h_ref[rows, :] = h
        u_ref[rows, :] = _rms(h, gmix_ref[...]).astype(BF16)
    _cast_slabs(refs[:n_cast], refs[n_cast + 2:])


def _ffn1(x, g1, wg, wu, wd, gmix, tm, cast_weights):
    t, d = x.shape
    ff = wg.shape[1]
    tile = pl.BlockSpec((tm, d), lambda i: (i, 0))
    steps = t // tm
    c_in, c_out, c_shape = _cast_plan(cast_weights, steps)
    outs = pl.pallas_call(
        _ffn1_kernel,
        grid=(steps,),
        in_specs=[tile, _resident((1, d)), _resident((d, ff)), _resident((d, ff)),
                  _resident((ff, d)), _resident((1, d))] + c_in,
        out_specs=[tile, tile] + c_out,
        out_shape=[jax.ShapeDtypeStruct((t, d), F32), jax.ShapeDtypeStruct((t, d), BF16)] + c_shape,
        compiler_params=_cparams(1),
        name="ffn1",
    )(x, g1, wg, wu, wd, gmix, *cast_weights)
    return outs[0], outs[1], outs[2:]


def _rotary(t, cos, sin_signed):
    return t * cos + pltpu.roll(t, HEAD_DIM // 2, axis=1) * sin_signed


CONV_OFF = CONV_HALO - (CONV_WIDTH - 1)
CONV_SLABS = 4
CONV_MXU_PHASES = (2, 3, 4, 5, 6, 7)


def _build_conv_diag(w_ref, dmat):
    eye = (lax.broadcasted_iota(jnp.int32, (LANES, LANES), 0)
           == lax.broadcasted_iota(jnp.int32, (LANES, LANES), 1))
    for c in range(dmat.shape[0]):
        for q in range(CONV_SLABS):
            for g, r in enumerate(CONV_MXU_PHASES):
                j = SUBLANES * q + r - CONV_OFF
                assert 0 <= j < CONV_WIDTH
                tap = w_ref[j:j + 1, c * LANES:(c + 1) * LANES]
                dmat[c, q * LANES:(q + 1) * LANES, g * LANES:(g + 1) * LANES] = (
                    jnp.where(eye, tap, 0.0).astype(BF16))


def _conv_lane_chunk(xbuf, w_ref, b_ref, dmat, acc_ref, tm, c):
    cols = slice(c * LANES, (c + 1) * LANES)
    taps = [w_ref[j:j + 1, cols] for j in range(CONV_WIDTH)]
    bias = b_ref[:, cols]
    m_rows = tm + BF16_ROWS
    slabs = [xbuf[SUBLANES * q:SUBLANES * q + m_rows, cols].astype(BF16) for q in range(CONV_SLABS)]
    phase_sums = _dot(jnp.concatenate(slabs, axis=1), dmat[c])
    for rb in range(tm // CONV_ROWS):
        base = rb * CONV_ROWS
        out = None
        for r in range(SUBLANES):
            if r in CONV_MXU_PHASES:
                g = CONV_MXU_PHASES.index(r)
                shifted = phase_sums[base + r:base + r + CONV_ROWS, g * LANES:(g + 1) * LANES]
            else:
                n = CONV_ROWS if r == 0 else CONV_ROWS + SUBLANES
                part = None
                for q in range((CONV_OFF + CONV_WIDTH - 1) // SUBLANES + 1):
                    j = SUBLANES * q + r - CONV_OFF
                    if 0 <= j < CONV_WIDTH:
                        lo = base + SUBLANES * q
                        term = taps[j] * xbuf[lo:lo + n, cols]
                        part = term if part is None else part + term
                shifted = part if r == 0 else part[r:r + CONV_ROWS]
            out = shifted if out is None else out + shifted
        acc_ref[base:base + CONV_ROWS, cols] = out + bias


def _mixer_in_kernel(u_ref, w_ref, cos_ref, sin_ref, wdw_ref, bdw_ref, gln_ref, bln_ref, wco_ref,
                     gyc_ref, q0_ref, q1_ref, q2_ref, k0_ref, k1_ref, k2_ref,
                     v0_ref, v1_ref, v2_ref, gb_ref, xbuf, acc_ref, ga_buf, rope_buf, perm, dmat,
                     *, tm, c_conv, d_model):
    dils = [dil for _, dil in GROUPS]

    @pl.when((pl.program_id(0) == 0) & (pl.program_id(1) == 0))
    def _():
        _build_conv_diag(wdw_ref, dmat)
        xbuf[CONV_HALO + tm:, :] = jnp.zeros((BF16_ROWS, c_conv), F32)
        dst = lax.broadcasted_iota(jnp.int32, (tm, tm), 0)
        tok = lax.broadcasted_iota(jnp.int32, (tm, tm), 1)
        for g, dil in enumerate(dils):
            if dil > 1:
                n = tm // dil
                perm[g] = jnp.where(tok == (dst % n) * dil + dst // n, 1.0, 0.0).astype(BF16)

    @pl.when(pl.program_id(1) == 0)
    def _():
        xbuf[0:CONV_HALO, :] = jnp.zeros((CONV_HALO, c_conv), F32)

    u = u_ref[...]

    def proj(lo, width, lhs=None):
        return _dot(u if lhs is None else lhs, w_ref[:, lo:lo + width])

    a = proj(0, c_conv)
    gate = proj(c_conv, c_conv)
    xbuf[CONV_HALO:CONV_HALO + tm, :] = a * _sigmoid(gate)

    base = 2 * c_conv
    gbase = base + 3 * ATTN_W

    tile_idx = pl.program_id(0) * pl.num_programs(1) + pl.program_id(1)
    n_tiles = pl.num_programs(0) * pl.num_programs(1)
    cos, sin = _unpack_rope(cos_ref, sin_ref, 2 * tile_idx >= n_tiles)
    rope_buf[0] = cos
    rope_buf[1] = sin

    def strided_rows(g):
        dil = dils[g]
        if dil == 1:
            return u, cos, sin
        n = tm // dil
        u_g = _dot(perm[g], u).astype(BF16)
        cos_g = jnp.concatenate([rope_buf[0, pl.ds(r, n, stride=dil), :] for r in range(dil)], axis=0)
        sin_g = jnp.concatenate([rope_buf[1, pl.ds(r, n, stride=dil), :] for r in range(dil)], axis=0)
        return u_g, cos_g, sin_g

    def emit(out_ref, val, dil):
        n = tm // dil
        for r in range(dil):
            out_ref[:, r * GROUP_W:(r + 1) * GROUP_W] = val[r * n:(r + 1) * n, :].astype(BF16)

    def group_pieces(g, q_ref, k_ref, v_ref):
        dil = dils[g]
        cache = []

        def inputs():
            if not cache:
                cache.append(strided_rows(g))
            return cache[0]

        def rotary_piece(out_ref, lo, scale):
            def piece():
                u_g, cos_g, sin_g = inputs()
                if scale is not None:
                    cos_g, sin_g = cos_g * scale, sin_g * scale
                t = proj(lo, GROUP_W, u_g)
                heads = [_rotary(t[:, h * HEAD_DIM:(h + 1) * HEAD_DIM], cos_g, sin_g)
                         for h in range(HEADS_PER_GROUP)]
                emit(out_ref, jnp.concatenate(heads, axis=1), dil)
            return piece

        def value_piece():
            emit(v_ref, proj(base + 2 * ATTN_W + g * GROUP_W, GROUP_W, inputs()[0]), dil)

        return [rotary_piece(q_ref, base + g * GROUP_W, HEAD_DIM ** -0.5 * LOG2_E),
                rotary_piece(k_ref, base + ATTN_W + g * GROUP_W, None), value_piece]

    def gate_a_piece():
        ga_buf[...] = _sigmoid(proj(gbase, d_model))

    def gate_b_piece():
        gb_ref[...] = _sigmoid(proj(gbase + d_model, d_model)).astype(BF16)

    pieces = [gate_a_piece, gate_b_piece]
    for g, refs in enumerate(((q0_ref, k0_ref, v0_ref), (q1_ref, k1_ref, v1_ref),
                              (q2_ref, k2_ref, v2_ref))):
        pieces += group_pieces(g, *refs)

    for c in range(c_conv // LANES):
        _conv_lane_chunk(xbuf, wdw_ref, bdw_ref, dmat, acc_ref, tm, c)
    xbuf[0:CONV_HALO, :] = xbuf[tm:tm + CONV_HALO, :]
    pieces[0]()

    acc = acc_ref[...]
    mu = jnp.mean(acc, axis=-1, keepdims=True)
    cen = acc - mu
    var = jnp.mean(cen * cen, axis=-1, keepdims=True)
    y = cen * lax.rsqrt(var + EPS) * gln_ref[...] + bln_ref[...]
    y = (y * _sigmoid(y)).astype(BF16)
    gyc_ref[...] = (ga_buf[...] * _dot(y, wco_ref[...])).astype(BF16)

    for piece in pieces[1:]:
        piece()


def _mixer_in(u, w_in, cos, sin, w_dw, b_dw, g_ln, b_ln, w_co, batch, tm):
    t, d = u.shape
    in_w = w_in.shape[1]
    c_conv = w_dw.shape[1]
    per_seq = t // batch // tm
    idx = lambda b, i: (b * per_seq + i, 0)
    row = lambda w: pl.BlockSpec((tm, w), idx)
    half_tiles = t // 2 // tm
    rope = pl.BlockSpec((tm, HEAD_DIM), lambda b, i: (lax.rem(b * per_seq + i, half_tiles), 0))
    strided = lambda dil: pl.BlockSpec((tm // dil, dil * GROUP_W), idx)
    strided_sds = lambda dil: jax.ShapeDtypeStruct((t // dil, dil * GROUP_W), BF16)
    dils = [dil for _, dil in GROUPS]
    out_specs = [row(d)] + [strided(dil) for _ in range(3) for dil in dils] + [row(d)]
    out_shape = ([jax.ShapeDtypeStruct((t, d), BF16)]
                 + [strided_sds(dil) for _ in range(3) for dil in dils]
                 + [jax.ShapeDtypeStruct((t, d), BF16)])
    return pl.pallas_call(
        functools.partial(_mixer_in_kernel, tm=tm, c_conv=c_conv, d_model=d),
        grid=(batch, per_seq),
        in_specs=[row(d), _resident((d, in_w)), rope, rope,
                  _resident((CONV_WIDTH, c_conv)), _resident((1, c_conv)), _resident((1, c_conv)),
                  _resident((1, c_conv)), _resident((c_conv, d))],
        out_specs=out_specs,
        out_shape=out_shape,
        scratch_shapes=[pltpu.VMEM((CONV_HALO + tm + BF16_ROWS, c_conv), F32),
                        pltpu.VMEM((tm, c_conv), F32),
                        pltpu.VMEM((tm, d), F32),
                        pltpu.VMEM((2, tm, HEAD_DIM), F32),
                        pltpu.VMEM((N_GROUPS, tm, tm), BF16),
                        pltpu.VMEM((c_conv // LANES, CONV_SLABS * LANES,
                                    len(CONV_MXU_PHASES) * LANES), BF16)],
        compiler_params=_cparams(2),
        name="mixer_in",
    )(u, w_in, cos, sin, w_dw, b_dw, g_ln, b_ln, w_co)


def _attn_kernel(q_ref, k_ref, v_ref, kh_ref, vh_ref, o_ref, lse_ref, kbuf, vbuf,
                 *, lt, unroll, rps):
    first_tile = pl.program_id(2) == 0
    ones = jnp.ones((lt + Q_BLOCK, HEAD_DIM), BF16)
    for rr in range(rps):
        kbuf[rr, 0:Q_BLOCK, :] = kh_ref[0, :, rr * GROUP_W:(rr + 1) * GROUP_W]
        kbuf[rr, Q_BLOCK:, :] = k_ref[0, :, rr * GROUP_W:(rr + 1) * GROUP_W]
        for h in range(HEADS_PER_GROUP):
            src = slice(rr * GROUP_W + h * HEAD_DIM, rr * GROUP_W + (h + 1) * HEAD_DIM)
            vbuf[rr, 0:Q_BLOCK, 2 * h * HEAD_DIM:(2 * h + 1) * HEAD_DIM] = vh_ref[0, :, src]
            vbuf[rr, Q_BLOCK:, 2 * h * HEAD_DIM:(2 * h + 1) * HEAD_DIM] = v_ref[0, :, src]
            vbuf[rr, :, (2 * h + 1) * HEAD_DIM:(2 * h + 2) * HEAD_DIM] = ones

    qi = lax.broadcasted_iota(jnp.int32, (Q_BLOCK, 2 * Q_BLOCK), 0)
    ki = lax.broadcasted_iota(jnp.int32, (Q_BLOCK, 2 * Q_BLOCK), 1)
    dist = qi + Q_BLOCK - ki
    band = (dist >= 0) & (dist <= Q_BLOCK)
    lane = lax.broadcasted_iota(jnp.int32, (Q_BLOCK, LANES), 1)
    den_lane = lane % STAT_REP >= STAT_REP // 2
    nt = (((1,), (1,)), ((), ()))

    def blocks(units):
        work = [(rr, start, mask, h) for rr, start, mask in units for h in range(HEADS_PER_GROUP)]
        col = lambda rr, h: slice(rr * GROUP_W + h * HEAD_DIM, rr * GROUP_W + (h + 1) * HEAD_DIM)
        s = [lax.dot_general(q_ref[0, pl.ds(start, Q_BLOCK), col(rr, h)],
                             kbuf[rr, pl.ds(start, 2 * Q_BLOCK), col(0, h)], nt,
                             preferred_element_type=F32) for rr, start, _, h in work]
        s = [jnp.where(mask, x, NEG) for x, (_, _, mask, _) in zip(s, work)]
        m = [jnp.max(x, axis=-1, keepdims=True) for x in s]
        e = [jnp.exp2(x - mx).astype(BF16) for x, mx in zip(s, m)]
        pv = [_dot(ex, vbuf[rr, pl.ds(start, 2 * Q_BLOCK), 2 * h * HEAD_DIM:(2 * h + 2) * HEAD_DIM])
              for ex, (rr, start, _, h) in zip(e, work)]
        for u, (rr, start, _) in enumerate(units):
            rows = pl.ds(start, Q_BLOCK)
            packed = None
            for h in range(HEADS_PER_GROUP):
                i = u * HEADS_PER_GROUP + h
                den = pv[i][:, HEAD_DIM:]
                o_ref[0, rows, col(rr, h)] = pv[i][:, :HEAD_DIM].astype(BF16)
                stat = jnp.where(den_lane, den, m[i])
                packed = stat if packed is None else jnp.where(lane >= h * STAT_REP, stat, packed)
            lse_ref[0, rows, rr * LANES:(rr + 1) * LANES] = packed

    def run(units):
        for lo in range(0, len(units), unroll):
            blocks(units[lo:lo + unroll])

    per = lt // Q_BLOCK
    upr = min(unroll, per)
    first_mask = band & jnp.logical_or(ki >= Q_BLOCK, jnp.logical_not(first_tile))
    run([(rr, j * Q_BLOCK, first_mask if j == 0 else band)
         for rr in range(rps) for j in range(upr)])

    def body(n, carry):
        start = n * (upr * Q_BLOCK)
        run([(rr, pl.multiple_of(start + j * Q_BLOCK, Q_BLOCK), band)
             for rr in range(rps) for j in range(upr)])
        return carry

    lax.fori_loop(1, per // upr, body, 0)


def _attention_group(q, k, v, batch, dilation, lt, rps):
    rows, _ = q.shape
    l = rows // batch
    lt = min(lt, l)
    per = lt // Q_BLOCK
    unroll = min(ATTN_UNROLL, per * rps)
    assert l % lt == 0 and dilation % rps == 0 and per % min(unroll, per) == 0
    view = lambda x: x.reshape(batch, l, x.shape[-1])
    tile = lambda width: pl.BlockSpec((1, lt, rps * width), lambda bi, r, i: (bi, i, r))
    halo = pl.BlockSpec((1, Q_BLOCK, rps * GROUP_W),
                        lambda bi, r, i: (bi, jnp.maximum(i * per - 1, 0), r))
    o, lse = pl.pallas_call(
        functools.partial(_attn_kernel, lt=lt, unroll=unroll, rps=rps),
        grid=(batch, dilation // rps, l // lt),
        in_specs=[tile(GROUP_W), tile(GROUP_W), tile(GROUP_W), halo, halo],
        out_specs=[tile(GROUP_W), tile(LANES)],
        out_shape=[jax.ShapeDtypeStruct((batch, l, dilation * GROUP_W), BF16),
                   jax.ShapeDtypeStruct((batch, l, dilation * LANES), F32)],
        scratch_shapes=[pltpu.VMEM((rps, lt + Q_BLOCK, GROUP_W), BF16),
                        pltpu.VMEM((rps, lt + Q_BLOCK, 2 * GROUP_W), BF16)],
        compiler_params=_cparams(3),
        name=f"attn_d{dilation}",
    )(view(q), view(k), view(v), view(k), view(v))
    return o.reshape(rows, dilation * GROUP_W), lse.reshape(rows, dilation * LANES)


def _tail_kernel(h_ref, gyc_ref, gb_ref, o0_ref, o1_ref, o2_ref, l0_ref, l1_ref, l2_ref, p_ref,
                 wao_ref, wmix_ref, g2_ref, wg_ref, wu_ref, wd_ref, gple_ref, wpg_ref, wpp_ref,
                 gfin_ref, y_ref, obuf, lbuf, spread, *, tm, final_norm):
    @pl.when(pl.program_id(0) == 0)
    def _():
        src = lax.broadcasted_iota(jnp.int32, spread.shape, 0) % LANES
        dst_head = lax.broadcasted_iota(jnp.int32, spread.shape, 1) // HEAD_DIM
        spread[...] = jnp.where(src == dst_head * STAT_REP, 1.0, 0.0).astype(BF16)

    parts = [_mixer_out_rows(part, tm // TAIL_PARTS, h_ref, gyc_ref, gb_ref,
                             (o0_ref, o1_ref, o2_ref), (l0_ref, l1_ref, l2_ref),
                             wao_ref, wmix_ref, obuf, lbuf, spread) for part in range(TAIL_PARTS)]
    h = parts[0] if TAIL_PARTS == 1 else jnp.concatenate(parts, axis=0)
    h = _swiglu_half_step(h, g2_ref[...], wg_ref, wu_ref, wd_ref)
    gate = _sigmoid(_dot(_rms(h, gple_ref[...]).astype(BF16), wpg_ref[...]))
    h = h + gate * _dot(p_ref[...].astype(BF16), wpp_ref[...])
    y_ref[...] = _rms(h, gfin_ref[...]) if final_norm else h


def _mixer_out_rows(part, tp, h_ref, gyc_ref, gb_ref, o_refs, l_refs, wao_ref, wmix_ref,
                    obuf, lbuf, spread):
    rows_tok = slice(part * tp, (part + 1) * tp)

    def token_order(dst, src_ref, dil, width):
        chunks = width // LANES
        n = tp // dil
        src_rows = slice(part * n, (part + 1) * n)
        if dil == 1:
            return [src_ref[src_rows, c * LANES:(c + 1) * LANES].astype(F32) for c in range(chunks)]
        for r in range(dil):
            for c in range(chunks):
                lo = r * width + c * LANES
                dst[c, pl.ds(part * tp + r, n, stride=dil), :] = (
                    src_ref[src_rows, lo:lo + LANES].astype(F32))
        return [dst[c, rows_tok, :] for c in range(chunks)]

    dils = [dil for _, dil in GROUPS]
    ls = [token_order(lbuf.at[g], ref, dils[g], LANES)[0] for g, ref in enumerate(l_refs)]
    os_ = [token_order(obuf.at[g], ref, dils[g], GROUP_W) for g, ref in enumerate(o_refs)]

    half = STAT_REP // 2
    lane = lax.broadcasted_iota(jnp.int32, ls[0].shape, 1)
    m = jnp.maximum(jnp.maximum(ls[0], ls[1]), ls[2])
    es = [jnp.exp2(x - m) for x in ls]
    dens = [pltpu.roll(x, LANES - half, axis=1) for x in ls]
    inv = 1.0 / (es[0] * dens[0] + es[1] * dens[1] + es[2] * dens[2])
    alphas = [jnp.where(lane % STAT_REP < half, x * inv, 0.0) for x in es]
    wide = []
    for a in alphas:
        hi = a.astype(BF16)
        lo = (a - hi.astype(F32)).astype(BF16)
        wide.append(_dot(jnp.concatenate([hi, lo], axis=1), spread[...]))
    heads = []
    for h in range(HEADS_PER_GROUP):
        acc = None
        for g in range(N_GROUPS):
            term = wide[g][:, h * HEAD_DIM:(h + 1) * HEAD_DIM] * os_[g][h]
            acc = term if acc is None else acc + term
        heads.append(acc.astype(BF16))
    o = jnp.concatenate(heads, axis=1)
    y_attn = _dot(o, wao_ref[...])
    mixed = gyc_ref[rows_tok, :].astype(F32) + gb_ref[rows_tok, :].astype(F32) * y_attn
    return h_ref[rows_tok, :] + _dot(mixed.astype(BF16), wmix_ref[...])


def _tail(h, gyc, gb, os_, ls_, p, wao, wmix, g2, wg, wu, wd, gple, wpg, wpp, gfin, tm,
          final_norm):
    t, d = h.shape
    row = lambda w: pl.BlockSpec((tm, w), lambda i: (i, 0))
    strided = lambda dil, w: pl.BlockSpec((tm // dil, dil * w), lambda i: (i, 0))
    res = lambda x: _resident(x.shape)
    weights = (wao, wmix, g2, wg, wu, wd, gple, wpg, wpp, gfin)
    dils = [dil for _, dil in GROUPS]
    return pl.pallas_call(
        functools.partial(_tail_kernel, tm=tm, final_norm=final_norm),
        grid=(t // tm,),
        in_specs=[row(d), row(d), row(d)] + [strided(dil, GROUP_W) for dil in dils]
                 + [strided(dil, LANES) for dil in dils]
                 + [row(p.shape[1])] + [res(x) for x in weights],
        out_specs=row(d),
        out_shape=jax.ShapeDtypeStruct((t, d), F32),
        scratch_shapes=[pltpu.VMEM((N_GROUPS, GROUP_W // LANES, tm, LANES), F32),
                        pltpu.VMEM((N_GROUPS, 1, tm, LANES), F32),
                        pltpu.VMEM((2 * LANES, GROUP_W), BF16)],
        compiler_params=_cparams(1),
        name="tail",
    )(h, gyc, gb, *os_, *ls_, p, *weights)


def kernel(x, p, positions, g_ffn1, w_ffn1_gate, w_ffn1_up, w_ffn1_down, g_mix, w_in,
           w_dw, b_dw, g_conv_ln, b_conv_ln, w_conv_out, w_attn_out, w_mix_out,
           g_ffn2, w_ffn2_gate, w_ffn2_up, w_ffn2_down, g_ple, w_ple_gate, w_ple_proj,
           g_final):
    b, s, d = x.shape
    depth = p.shape[0]
    t = b * s
    c_conv = w_dw.shape[-1]
    assert w_in.shape[-1] == 2 * c_conv + 3 * ATTN_W + 2 * d
    tm, tm_mix = 512, 256
    max_dil = max(dil for _, dil in GROUPS)
    assert s % tm == 0 and s % tm_mix == 0
    assert tm_mix % CONV_ROWS == 0 and (tm_mix // max_dil) % 16 == 0
    for window, dil in GROUPS:
        assert window // dil == Q_BLOCK and (s // dil) % Q_BLOCK == 0

    assert depth == 1
    i = 0
    row = lambda v: v.reshape(1, -1)
    cos, sin, (wg1, wu1, wd1) = _rope_tables(
        positions, ROPE_ROWS, [w_ffn1_gate[i], w_ffn1_up[i], w_ffn1_down[i]])
    h, u, (win, wco, wao, wmix, wg2, wu2, wd2, wpg, wpp) = _ffn1(
        x.reshape(t, d), row(g_ffn1[i]), wg1, wu1, wd1, row(g_mix[i]), tm,
        [w_in[i], w_conv_out[i], w_attn_out[i], w_mix_out[i], w_ffn2_gate[i], w_ffn2_up[i],
         w_ffn2_down[i], w_ple_gate[i], w_ple_proj[i]])
    gyc, q0, q1, q2, k0, k1, k2, v0, v1, v2, gb = _mixer_in(
        u, win, cos, sin, w_dw[i], row(b_dw[i]), row(g_conv_ln[i]), row(b_conv_ln[i]), wco,
        b, tm_mix)
    os_, ls_ = [], []
    for (_, dil), q, k, v in zip(GROUPS, (q0, q1, q2), (k0, k1, k2), (v0, v1, v2)):
        o, lse = _attention_group(q, k, v, b, dil, ATTN_ROWS, max(1, ATTN_ROWS * dil // s))
        os_.append(o)
        ls_.append(lse)
    h = _tail(h, gyc, gb, os_, ls_, p[i].reshape(t, -1), wao, wmix, row(g_ffn2[i]), wg2, wu2,
              wd2, row(g_ple[i]), wpg, wpp, row(g_final), tm, True)
    return h.reshape(b, s, d)
```
